```python
import math
import jax
import jax.numpy as jnp
from jax import lax
import numpy as np

D_MODEL = 1024
BATCH = 4
SEQ = 8192
DEPTH = 4
DEC_BATCH = 32
DEC_SEQ = 32
PAST_LEN = 1024

CHUNK = 64
Q_BLOCK = 128
N_GROUPS = 4
N_GROUP_HEADS = 4
GROUP_WIDTH = D_MODEL // N_GROUPS
HEAD_DIM = GROUP_WIDTH // N_GROUP_HEADS
MIX_WIDTH = N_GROUPS * GROUP_WIDTH
DIFF_DH = HEAD_DIM // 2
N_IDX_HEADS = 8
IDX_DIM = 64
TOPK_MAX = 256
T5_BUCKETS = 32
T5_MAX_DIST = 128
N_T5_HEADS = 2 * N_GROUP_HEADS
D_FF = 2816
CONV_W = 3
RMS_EPS = 1e-6
SUBLN_EPS = 1e-5
FORGET_BIAS_INIT = 3.0
PROJ_SIZES = (GROUP_WIDTH, GROUP_WIDTH, GROUP_WIDTH,
              GROUP_WIDTH, GROUP_WIDTH, GROUP_WIDTH,
              GROUP_WIDTH, GROUP_WIDTH, GROUP_WIDTH,
              N_IDX_HEADS * IDX_DIM, IDX_DIM, N_IDX_HEADS,
              GROUP_WIDTH, GROUP_WIDTH, GROUP_WIDTH,
              N_GROUP_HEADS)
PROJ_WIDTH = sum(PROJ_SIZES)

kernel_name = 'hybrid_streaming_encoder_step'


def rmsnorm(x, g, eps=RMS_EPS):
    xf = x.astype(jnp.float32)
    y = xf * lax.rsqrt(jnp.mean(xf * xf, axis=-1, keepdims=True) + eps)
    return (y * g.astype(jnp.float32)).astype(x.dtype)


def t5_bucket(rel):
    half = T5_BUCKETS // 2
    max_exact = half // 2
    base = jnp.where(rel > 0, half, 0)
    n = jnp.abs(rel)
    nf = jnp.maximum(n, 1).astype(jnp.float32)
    large = max_exact + (jnp.log(nf / max_exact) / math.log(T5_MAX_DIST / max_exact)
                         * (half - max_exact)).astype(jnp.int32)
    large = jnp.minimum(large, half - 1)
    return base + jnp.where(n < max_exact, n, large)


def split_proj(p):
    outs = []
    o = 0
    for s in PROJ_SIZES:
        outs.append(p[..., o:o + s])
        o += s
    return outs


def stack_rows(rows, i):
    return jnp.stack([r[i] for r in rows], axis=0)


def mix_block(qb, keys, lam, lam_init, subln_g, t5_table, topk):
    q_sb, q_df, q_ds, q_ix, w_ix, q_fx, c_q, pos_q = qb
    k_sb, v_sb, k_df, v_df, k_ds, v_ds, k_ix, k_fx, v_fx, c_k, pos_k = keys
    f32 = jnp.float32
    B, Tq = q_sb.shape[0], q_sb.shape[1]
    rel = pos_k[None, :] - pos_q[:, None]
    chunk_ok = (pos_k[None, :] // CHUNK) <= (pos_q[:, None] // CHUNK)
    scale = HEAD_DIM ** -0.5

    z = jnp.einsum('bqhd,bkhd->bhqk', q_sb.astype(f32), k_sb) * scale
    strict = rel < 0
    log_keep = jnp.where(strict, jax.nn.log_sigmoid(-z), 0.0)
    log_between = lax.cumsum(log_keep, axis=3, reverse=True) - log_keep
    a_sb = jnp.where(strict, jnp.exp(jax.nn.log_sigmoid(z) + log_between), 0.0)
    o_sb = jnp.einsum('bhqk,bkhd->bqhd', a_sb, v_sb)

    tab_df = t5_table[:, :N_GROUP_HEADS].astype(f32).T
    bias_df = tab_df[:, t5_bucket(rel)]
    s_df = jnp.einsum('bqhcd,bkhcd->bchqk', q_df.astype(f32), k_df) * (DIFF_DH ** -0.5) + bias_df
    p_df = jax.nn.softmax(jnp.where(chunk_ok, s_df, -jnp.inf), axis=-1)
    o_df = jnp.einsum('bhqk,bkhd->bqhd', p_df[:, 0] - lam * p_df[:, 1], v_df)
    o_df = rmsnorm(o_df, subln_g, SUBLN_EPS) * (1.0 - lam_init)

    s_ix = jax.nn.relu(jnp.einsum('bqhd,bkd->bqhk', q_ix.astype(f32), k_ix) * (IDX_DIM ** -0.5))
    score = jnp.einsum('bqh,bqhk->bqk', w_ix.astype(f32) * (N_IDX_HEADS ** -0.5), s_ix)
    score = jnp.where(chunk_ok, score, -jnp.inf)
    top_val, top_idx = lax.top_k(score, topk)
    valid = top_val > -jnp.inf
    take = jax.vmap(lambda a, i: a[i])
    k_sel = take(k_ds, top_idx)
    v_sel = take(v_ds, top_idx)
    rel_sel = pos_k[top_idx] - pos_q[None, :, None]
    tab_ds = t5_table[:, N_GROUP_HEADS:].astype(f32).T
    bias_ds = jnp.moveaxis(tab_ds[:, t5_bucket(rel_sel)], 0, 1)
    s_ds = jnp.einsum('bqhd,bqkhd->bhqk', q_ds.astype(f32), k_sel) * scale + bias_ds
    p_ds = jax.nn.softmax(jnp.where(valid[:, None], s_ds, -jnp.inf), axis=-1)
    o_ds = jnp.einsum('bhqk,bqkhd->bqhd', p_ds, v_sel)

    decay = jnp.moveaxis(c_q, -1, 1)[..., :, None] - jnp.moveaxis(c_k, -1, 1)[..., None, :]
    s_fx = jnp.einsum('bqhd,bkhd->bhqk', q_fx.astype(f32), k_fx) * scale + decay
    p_fx = jax.nn.softmax(jnp.where(rel <= 0, s_fx, -jnp.inf), axis=-1)
    o_fx = jnp.einsum('bhqk,bkhd->bqhd', p_fx, v_fx)

    return jnp.concatenate([o.reshape(B, Tq, GROUP_WIDTH) for o in (o_sb, o_df, o_ds, o_fx)], axis=-1)


def trunk_layer(x, past, lw, lam_init, t5_table):
    (g_mix, w_in, b_forget, w_out, lam_q1, lam_k1, lam_q2, lam_k2, diff_subln,
     g_ffn, w_gate, w_up, w_down, conv_w, conv_b) = lw
    (pk_sb, pv_sb, pk_df, pv_df, pk_ds, pv_ds, pk_ix, pk_fx, pv_fx, plogf, pconv) = past
    f32 = jnp.float32
    B, T, _ = x.shape
    P = pk_sb.shape[1]
    L = P + T
    topk = min(TOPK_MAX, L // 4)
    H, D = N_GROUP_HEADS, HEAD_DIM

    hn = rmsnorm(x, g_mix)
    parts = split_proj(jnp.einsum('btd,dn->btn', hn, w_in))
    hd = lambda a: a.reshape(B, T, H, D)
    q_sb, k_sb, v_sb = hd(parts[0]), hd(parts[1]), hd(parts[2])
    q_df = parts[3].reshape(B, T, H, 2, DIFF_DH)
    k_df = parts[4].reshape(B, T, H, 2, DIFF_DH)
    v_df = hd(parts[5])
    q_ds, k_ds, v_ds = hd(parts[6]), hd(parts[7]), hd(parts[8])
    q_ix = parts[9].reshape(B, T, N_IDX_HEADS, IDX_DIM)
    k_ix, w_ix = parts[10], parts[11]
    q_fx, k_fx, v_fx = hd(parts[12]), hd(parts[13]), hd(parts[14])
    logf = jax.nn.log_sigmoid(parts[15].astype(f32) + b_forget.astype(f32))

    cat = lambda old, new: jnp.concatenate([old.astype(f32), new.astype(f32)], axis=1)
    c_all = jnp.cumsum(cat(plogf, logf), axis=1)
    pos_k = jnp.arange(L, dtype=jnp.int32)
    pos_q = pos_k[P:]
    keys = (cat(pk_sb, k_sb), cat(pv_sb, v_sb), cat(pk_df, k_df), cat(pv_df, v_df),
            cat(pk_ds, k_ds), cat(pv_ds, v_ds), cat(pk_ix, k_ix), cat(pk_fx, k_fx),
            cat(pv_fx, v_fx), c_all, pos_k)
    lam = (jnp.exp(jnp.sum(lam_q1.astype(f32) * lam_k1.astype(f32)))
           - jnp.exp(jnp.sum(lam_q2.astype(f32) * lam_k2.astype(f32))) + lam_init)
    qs = (q_sb, q_df, q_ds, q_ix, w_ix, q_fx, c_all[:, P:])
    run = lambda qb: mix_block(qb, keys, lam, lam_init, diff_subln, t5_table, topk)
    if T % Q_BLOCK == 0:
        nb = T // Q_BLOCK
        blk = lambda a: jnp.moveaxis(a.reshape(B, nb, Q_BLOCK, *a.shape[2:]), 1, 0)
        out = lax.map(run, tuple(blk(a) for a in qs) + (pos_q.reshape(nb, Q_BLOCK),))
        mixed = jnp.moveaxis(out, 0, 1).reshape(B, T, MIX_WIDTH)
    else:
        mixed = run(qs + (pos_q,))
    x = x + jnp.einsum('btm,md->btd', mixed.astype(x.dtype), w_out)

    hn = rmsnorm(x, g_ffn)
    u = jnp.einsum('btd,df->btf', hn, w_gate)
    u_pad = jnp.concatenate([pconv.astype(u.dtype), u], axis=1)
    conv = conv_b
    for i in range(CONV_W):
        conv = conv + conv_w[i] * u_pad[:, i:i + T]
    hidden = jax.nn.silu(conv) * jnp.einsum('btd,df->btf', hn, w_up)
    x = x + jnp.einsum('btf,fd->btd', hidden, w_down)

    new_rows = (k_sb, v_sb, k_df, v_df, k_ds, v_ds, k_ix, k_fx, v_fx, logf, u_pad[:, T:])
    return x, new_rows


def setup_inputs(seed: int = 0) -> dict:
    key = jax.random.key(seed)
    ks = jax.random.split(key, 32)
    f32 = jnp.float32
    nrm = lambda k, shape, s=1.0: s * jax.random.normal(k, shape, f32)
    H, D = N_GROUP_HEADS, HEAD_DIM
    cshape = (DEPTH, DEC_BATCH, PAST_LEN, H, D)
    return {
        'x_prompt': nrm(ks[0], (BATCH, SEQ, D_MODEL)),
        'x_sample': nrm(ks[1], (DEC_BATCH, DEC_SEQ, D_MODEL)),
        'cache_sb_k': nrm(ks[2], cshape),
        'cache_sb_v': nrm(ks[3], cshape),
        'cache_diff_k': nrm(ks[4], (DEPTH, DEC_BATCH, PAST_LEN, H, 2, DIFF_DH)),
        'cache_diff_v': nrm(ks[5], cshape),
        'cache_dsa_k': nrm(ks[6], cshape),
        'cache_dsa_v': nrm(ks[7], cshape),
        'cache_dsa_kidx': nrm(ks[8], (DEPTH, DEC_BATCH, PAST_LEN, IDX_DIM)),
        'cache_fox_k': nrm(ks[9], cshape),
        'cache_fox_v': nrm(ks[10], cshape),
        'cache_fox_logf': jax.nn.log_sigmoid(FORGET_BIAS_INIT + nrm(ks[11], (DEPTH, DEC_BATCH, PAST_LEN, H))),
        'state_ffn_conv': nrm(ks[12], (DEPTH, DEC_BATCH, CONV_W - 1, D_FF)),
        'g_mix': 1.0 + nrm(ks[13], (DEPTH, D_MODEL), 0.02),
        'w_in': nrm(ks[14], (DEPTH, D_MODEL, PROJ_WIDTH), D_MODEL ** -0.5),
        'b_forget': FORGET_BIAS_INIT + nrm(ks[15], (DEPTH, H), 0.1),
        'w_out': nrm(ks[16], (DEPTH, MIX_WIDTH, D_MODEL), MIX_WIDTH ** -0.5),
        'lam_q1': nrm(ks[17], (DEPTH, DIFF_DH), 0.1),
        'lam_k1': nrm(ks[18], (DEPTH, DIFF_DH), 0.1),
        'lam_q2': nrm(ks[19], (DEPTH, DIFF_DH), 0.1),
        'lam_k2': nrm(ks[20], (DEPTH, DIFF_DH), 0.1),
        'diff_subln': 1.0 + nrm(ks[21], (DEPTH, HEAD_DIM), 0.02),
        'g_ffn': 1.0 + nrm(ks[22], (DEPTH, D_MODEL), 0.02),
        'w_gate': nrm(ks[23], (DEPTH, D_MODEL, D_FF), D_MODEL ** -0.5),
        'w_up': nrm(ks[24], (DEPTH, D_MODEL, D_FF), D_MODEL ** -0.5),
        'w_down': nrm(ks[25], (DEPTH, D_FF, D_MODEL), D_FF ** -0.5),
        'conv_w': nrm(ks[26], (DEPTH, CONV_W, D_FF), CONV_W ** -0.5),
        'conv_b': nrm(ks[27], (DEPTH, D_FF), 0.01),
        't5_table': nrm(ks[28], (T5_BUCKETS, N_T5_HEADS), 0.5),
        'g_final': 1.0 + nrm(ks[29], (D_MODEL,), 0.02),
    }


def reference(x_prompt, x_sample, cache_sb_k, cache_sb_v, cache_diff_k, cache_diff_v,
              cache_dsa_k, cache_dsa_v, cache_dsa_kidx, cache_fox_k, cache_fox_v, cache_fox_logf,
              state_ffn_conv, g_mix, w_in, b_forget, w_out, lam_q1, lam_k1, lam_q2, lam_k2,
              diff_subln, g_ffn, w_gate, w_up, w_down, conv_w, conv_b, t5_table, g_final):
    H, D = N_GROUP_HEADS, HEAD_DIM
    Bp = x_prompt.shape[0]
    dt = x_prompt.dtype
    empty_past = (jnp.zeros((Bp, 0, H, D), dt), jnp.zeros((Bp, 0, H, D), dt),
                  jnp.zeros((Bp, 0, H, 2, DIFF_DH), dt), jnp.zeros((Bp, 0, H, D), dt),
                  jnp.zeros((Bp, 0, H, D), dt), jnp.zeros((Bp, 0, H, D), dt),
                  jnp.zeros((Bp, 0, IDX_DIM), dt), jnp.zeros((Bp, 0, H, D), dt),
                  jnp.zeros((Bp, 0, H, D), dt), jnp.zeros((Bp, 0, H), jnp.float32),
                  jnp.zeros((Bp, CONV_W - 1, D_FF), dt))
    xp, xs = x_prompt, x_sample
    p_rows = []
    s_rows = []
    for l in range(DEPTH):
        lw = (g_mix[l], w_in[l], b_forget[l], w_out[l], lam_q1[l], lam_k1[l], lam_q2[l], lam_k2[l],
              diff_subln[l], g_ffn[l], w_gate[l], w_up[l], w_down[l], conv_w[l], conv_b[l])
        lam_init = 0.8 - 0.6 * math.exp(-0.3 * l)
        xp, rp = trunk_layer(xp, empty_past, lw, lam_init, t5_table)
        s_past = (cache_sb_k[l], cache_sb_v[l], cache_diff_k[l], cache_diff_v[l],
                  cache_dsa_k[l], cache_dsa_v[l], cache_dsa_kidx[l], cache_fox_k[l],
                  cache_fox_v[l], cache_fox_logf[l], state_ffn_conv[l])
        xs, rs = trunk_layer(xs, s_past, lw, lam_init, t5_table)
        p_rows.append(rp)
        s_rows.append(rs)
    y_prompt = rmsnorm(xp, g_final)
    y_sample = rmsnorm(xs, g_final)
    return (y_prompt, y_sample,
            stack_rows(p_rows, 0), stack_rows(p_rows, 1), stack_rows(p_rows, 2), stack_rows(p_rows, 3),
            stack_rows(p_rows, 4), stack_rows(p_rows, 5), stack_rows(p_rows, 6), stack_rows(p_rows, 7),
            stack_rows(p_rows, 8), stack_rows(p_rows, 9), stack_rows(p_rows, 10),
            stack_rows(s_rows, 0), stack_rows(s_rows, 1), stack_rows(s_rows, 2), stack_rows(s_rows, 3),
            stack_rows(s_rows, 4), stack_rows(s_rows, 5), stack_rows(s_rows, 6), stack_rows(s_rows, 7),
            stack_rows(s_rows, 8), stack_rows(s_rows, 9), stack_rows(s_rows, 10))
```

```python
import functools
import math

import jax
import jax.numpy as jnp
from jax import lax
from jax.experimental import pallas as pl
from jax.experimental.pallas import tpu as pltpu

F32 = jnp.float32
BF16 = jnp.bfloat16
I32 = jnp.int32

N_HEADS = 4
HEAD_DIM = 64
GROUP_WIDTH = N_HEADS * HEAD_DIM
DIFF_DH = HEAD_DIM // 2
N_IDX_HEADS = 8
IDX_DIM = 64
CHUNK = 64
TOPK_MAX = 256
T5_BUCKETS = 32
T5_MAX_DIST = 128
CONV_W = 3
RMS_EPS = 1e-6
SUBLN_EPS = 1e-5

LANES = 128
SUBLANES = 8
TK = 128
NEG = -1e30
SB_DEAD = -110.0
INT_MIN = -2 ** 31
KEY_NEG_INF = -2139095041
VMEM_LIMIT = 56 * 1024 * 1024

COL_QIX = 9 * GROUP_WIDTH
COL_SLAB = COL_QIX + N_IDX_HEADS * IDX_DIM
SLAB_WIX = IDX_DIM
SLAB_GATE = IDX_DIM + N_IDX_HEADS
COL_FX = COL_SLAB + LANES
PROJ_PAD = COL_FX + 3 * GROUP_WIDTH


def _cparams(sem):
    return pltpu.CompilerParams(dimension_semantics=sem, vmem_limit_bytes=VMEM_LIMIT)


def _resident(shape, index_map):
    return pl.BlockSpec(shape, index_map, pipeline_mode=pl.Buffered(1))


def _log_sigmoid(z):
    return jnp.minimum(z, 0.0) - jnp.log(1.0 + jnp.exp(-jnp.abs(z)))


def _dot_nt(a, b):
    return lax.dot_general(a, b, (((1,), (1,)), ((), ())), preferred_element_type=F32)


def _dot(a, b):
    return jnp.dot(a, b, preferred_element_type=F32)


def _proj_kernel(x_ref, g_ref, w_ref, bf_ref, y_ref, lf_ref):
    x = x_ref[...]
    hn = x * lax.rsqrt(jnp.mean(x * x, axis=-1, keepdims=True) + RMS_EPS) * g_ref[...]
    y = _dot(hn.astype(BF16), w_ref[...])
    y_ref[...] = y
    lf_ref[...] = _log_sigmoid(y[:, COL_SLAB:COL_SLAB + LANES] + bf_ref[...])


def _proj(x2, g, w, bf, tm):
    n, d = x2.shape
    return pl.pallas_call(
        _proj_kernel,
        grid=(n // tm,),
        in_specs=[pl.BlockSpec((tm, d), lambda i: (i, 0)),
                  pl.BlockSpec((1, d), lambda i: (0, 0)),
                  _resident((d, PROJ_PAD), lambda i: (0, 0)),
                  pl.BlockSpec((1, LANES), lambda i: (0, 0))],
        out_specs=[pl.BlockSpec((tm, PROJ_PAD), lambda i: (i, 0)),
                   pl.BlockSpec((tm, LANES), lambda i: (i, 0))],
        out_shape=[jax.ShapeDtypeStruct((n, PROJ_PAD), F32),
                   jax.ShapeDtypeStruct((n, LANES), F32)],
        compiler_params=_cparams(("parallel",)),
    )(x2, g, w, bf)


def _cumsum_kernel(x_ref, o_ref):
    nch, rows, _ = x_ref.shape
    lane = lax.broadcasted_iota(I32, (rows, LANES), 1)

    def body(c, carry):
        x = x_ref[c]
        d = 1
        while d < LANES:
            x = x + jnp.where(lane >= d, pltpu.roll(x, d, 1), 0.0)
            d *= 2
        x = x + carry
        o_ref[c] = x
        return jnp.broadcast_to(x[:, LANES - 1:LANES], (rows, LANES))

    lax.fori_loop(0, nch, body, jnp.zeros((rows, LANES), F32))


def _cumsum_seq(logf_all):
    b, l, h = logf_all.shape
    rows = b * h
    rows_pad = -(-rows // SUBLANES) * SUBLANES
    x = jnp.transpose(logf_all, (0, 2, 1)).reshape(rows, l // LANES, LANES)
    x = jnp.pad(jnp.transpose(x, (1, 0, 2)), ((0, 0), (0, rows_pad - rows), (0, 0)))
    c = pl.pallas_call(
        _cumsum_kernel,
        out_shape=jax.ShapeDtypeStruct(x.shape, F32),
        compiler_params=pltpu.CompilerParams(vmem_limit_bytes=VMEM_LIMIT),
    )(x)
    return jnp.transpose(c[:, :rows], (1, 0, 2)).reshape(b, h, l)


def _positions(q0, k0, tq):
    qpos = q0 + lax.broadcasted_iota(I32, (tq, TK), 0)
    kpos = k0 + lax.broadcasted_iota(I32, (tq, TK), 1)
    return qpos, kpos


def _chunk_visible(q0, k0, tq, l_keys):
    qpos, kpos = _positions(q0, k0, tq)
    return ((kpos // CHUNK) <= (qpos // CHUNK)) & (kpos < l_keys)


def _online_step(s, v, carry):
    m, l, acc = carry
    m_new = jnp.maximum(m, jnp.max(s, axis=-1, keepdims=True))
    alpha = jnp.exp(m - m_new)
    p = jnp.exp(s - m_new)
    l = alpha * l + jnp.sum(p, axis=-1, keepdims=True)
    acc = alpha * acc + _dot(p.astype(BF16), v)
    return m_new, l, acc


def _online_init(tq):
    return (jnp.full((tq, 1), NEG, F32), jnp.zeros((tq, 1), F32), jnp.zeros((tq, HEAD_DIM), F32))


def _tile_start(j):
    return pl.multiple_of(j * TK, TK)


def _sb_kernel(q_ref, k_ref, v_ref, o_ref, *, past, tq):
    q0 = past + pl.program_id(1) * tq
    n_vis = (q0 + tq + TK - 1) // TK
    n_int = q0 // TK
    later = (lax.broadcasted_iota(I32, (TK, TK), 0)
             > lax.broadcasted_iota(I32, (TK, TK), 1)).astype(BF16)

    for h in range(N_HEADS):
        q = q_ref[0, h]

        def tile(j, run, acc, masked):
            k0 = _tile_start(j)
            z = _dot_nt(q, k_ref[0, h, pl.ds(k0, TK), :])
            ls = _log_sigmoid(z)
            lk = ls - z
            if masked:
                qpos, kpos = _positions(q0, k0, tq)
                strict = kpos < qpos
                lk = jnp.where(strict, lk, 0.0)
            hi = lk.astype(BF16)
            rem = lk - hi.astype(F32)
            mid = rem.astype(BF16)
            lo = (rem - mid.astype(F32)).astype(BF16)
            between = _dot(hi, later) + _dot(mid, later) + _dot(lo, later) + run
            a = jnp.exp(ls + between)
            if masked:
                a = jnp.where(strict, a, 0.0)
            acc = acc + _dot(a.astype(BF16), v_ref[0, h, pl.ds(k0, TK), :])
            run = run + jnp.sum(lk, axis=-1, keepdims=True)
            return run, acc

        def masked_body(i, carry):
            return tile(n_vis - 1 - i, carry[0], carry[1], True)

        run, acc = lax.fori_loop(0, n_vis - n_int, masked_body,
                                 (jnp.zeros((tq, 1), F32), jnp.zeros((tq, HEAD_DIM), F32)))

        def cond(carry):
            return (carry[0] >= 0) & (carry[1] > SB_DEAD)

        def body(carry):
            j, _, run, acc = carry
            run, acc = tile(j, run, acc, False)
            return j - 1, jnp.max(run), run, acc

        _, _, _, acc = lax.while_loop(cond, body, (n_int - 1, jnp.max(run), run, acc))
        o_ref[0, :, h * HEAD_DIM:(h + 1) * HEAD_DIM] = acc


def _attn_specs(b_heads, tq, l_pad):
    q_spec = pl.BlockSpec((1, b_heads, tq, HEAD_DIM), lambda b, i: (b, 0, i, 0))
    kv_spec = _resident((1, N_HEADS, l_pad, HEAD_DIM), lambda b, i: (b, 0, 0, 0))
    o_spec = pl.BlockSpec((1, tq, GROUP_WIDTH), lambda b, i: (b, i, 0))
    return q_spec, kv_spec, o_spec


def _full(shape):
    return pl.BlockSpec(shape, lambda b, i: (0,) * len(shape))


def _sb_attn(q, k, v, past, tq):
    b, _, t, _ = q.shape
    l_pad = k.shape[2]
    q_spec, kv_spec, o_spec = _attn_specs(N_HEADS, tq, l_pad)
    return pl.pallas_call(
        functools.partial(_sb_kernel, past=past, tq=tq),
        grid=(b, t // tq),
        in_specs=[q_spec, kv_spec, kv_spec],
        out_specs=o_spec,
        out_shape=jax.ShapeDtypeStruct((b, t, GROUP_WIDTH), F32),
        compiler_params=_cparams(("parallel", "parallel")),
    )(q, k, v)


def _fx_kernel(q_ref, k_ref, v_ref, cq_ref, ck_ref, o_ref, *, past, tq):
    q0 = past + pl.program_id(1) * tq
    n_vis = (q0 + tq + TK - 1) // TK
    n_int = (q0 + 1) // TK

    for h in range(N_HEADS):
        q = q_ref[0, h]
        cq = cq_ref[0, :, h:h + 1]

        def tile(j, carry, masked):
            k0 = _tile_start(j)
            s = _dot_nt(q, k_ref[0, h, pl.ds(k0, TK), :])
            s = s + (cq - ck_ref[0, h:h + 1, pl.ds(k0, TK)])
            if masked:
                qpos, kpos = _positions(q0, k0, tq)
                s = jnp.where(kpos <= qpos, s, NEG)
            return _online_step(s, v_ref[0, h, pl.ds(k0, TK), :], carry)

        carry = lax.fori_loop(0, n_int, lambda j, c: tile(j, c, False), _online_init(tq))
        _, l, acc = lax.fori_loop(n_int, n_vis, lambda j, c: tile(j, c, True), carry)
        o_ref[0, :, h * HEAD_DIM:(h + 1) * HEAD_DIM] = acc / l


def _fx_attn(q, k, v, cq, ck, past, tq):
    b, _, t, _ = q.shape
    l_pad = k.shape[2]
    q_spec, kv_spec, o_spec = _attn_specs(N_HEADS, tq, l_pad)
    return pl.pallas_call(
        functools.partial(_fx_kernel, past=past, tq=tq),
        grid=(b, t // tq),
        in_specs=[q_spec, kv_spec, kv_spec,
                  pl.BlockSpec((1, tq, LANES), lambda b, i: (b, i, 0)),
                  _resident((1, N_HEADS, l_pad), lambda b, i: (b, 0, 0))],
        out_specs=o_spec,
        out_shape=jax.ShapeDtypeStruct((b, t, GROUP_WIDTH), F32),
        compiler_params=_cparams(("parallel", "parallel")),
    )(q, k, v, cq, ck)


def _biased_flash(q, k_ref, v_ref, h, bias_ref, bh, far_ref, q0, tq, l_keys, extra=None):
    jd = q0 // TK

    def scores(j):
        k0 = _tile_start(j)
        s = _dot_nt(q, k_ref[0, h, pl.ds(k0, TK), :])
        if extra is not None:
            s = s + extra(k0)
        return k0, s

    def far(j, carry):
        k0, s = scores(j)
        return _online_step(s + far_ref[bh:bh + 1, :], v_ref[0, h, pl.ds(k0, TK), :], carry)

    def near(j, carry):
        k0, s = scores(j)
        return _online_step(s + bias_ref[1, bh, 0:tq, :], v_ref[0, h, pl.ds(k0, TK), :], carry)

    carry = lax.fori_loop(0, jnp.maximum(jd - 1, 0), far, _online_init(tq))
    carry = lax.fori_loop(jnp.maximum(jd - 1, 0), jd, near, carry)
    k0, s = scores(jd)
    s = jnp.where(_chunk_visible(q0, k0, tq, l_keys), s + bias_ref[0, bh, 0:tq, :], NEG)
    _, l, acc = _online_step(s, v_ref[0, h, pl.ds(k0, TK), :], carry)
    return acc / l


def _df_kernel(q_ref, k_ref, v_ref, bias_ref, far_ref, lam_ref, g_ref, o_ref, *, past, tq, l_keys):
    q0 = past + pl.program_id(1) * tq
    lam_init = lam_ref[4:5, 0:1]
    lam = (jnp.exp(jnp.sum(lam_ref[0:1, :] * lam_ref[1:2, :], axis=-1, keepdims=True))
           - jnp.exp(jnp.sum(lam_ref[2:3, :] * lam_ref[3:4, :], axis=-1, keepdims=True)) + lam_init)
    for h in range(N_HEADS):
        o0 = _biased_flash(q_ref[0, 2 * h], k_ref, v_ref, h, bias_ref, h, far_ref, q0, tq, l_keys)
        o1 = _biased_flash(q_ref[0, 2 * h + 1], k_ref, v_ref, h, bias_ref, h, far_ref, q0, tq, l_keys)
        o = o0 - lam * o1
        o = o * lax.rsqrt(jnp.mean(o * o, axis=-1, keepdims=True) + SUBLN_EPS) * g_ref[...]
        o_ref[0, :, h * HEAD_DIM:(h + 1) * HEAD_DIM] = o * (1.0 - lam_init)


def _df_attn(q2, k, v, bias, far, lam_rows, subln, past, tq, l_keys):
    b, _, t, _ = q2.shape
    l_pad = k.shape[2]
    q_spec, kv_spec, o_spec = _attn_specs(2 * N_HEADS, tq, l_pad)
    return pl.pallas_call(
        functools.partial(_df_kernel, past=past, tq=tq, l_keys=l_keys),
        grid=(b, t // tq),
        in_specs=[q_spec, kv_spec, kv_spec, _full(bias.shape), _full(far.shape),
                  _full(lam_rows.shape), _full(subln.shape)],
        out_specs=o_spec,
        out_shape=jax.ShapeDtypeStruct((b, t, GROUP_WIDTH), F32),
        compiler_params=_cparams(("parallel", "parallel")),
    )(q2, k, v, bias, far, lam_rows, subln)


def _ds_kernel(qix_ref, wix_ref, kix_ref, q_ref, k_ref, v_ref, bias_ref, far_ref, o_ref,
               key_scr, msk_scr, *, past, tq, l_keys, topk):
    q0 = past + pl.program_id(1) * tq
    jd = q0 // TK
    n_vis = jd + 1
    wix = wix_ref[0] * (N_IDX_HEADS ** -0.5)

    def score_tile(j, masked):
        k0 = _tile_start(j)
        kix = kix_ref[0, pl.ds(k0, TK), :]
        sc = jnp.zeros((tq, TK), F32)
        for hh in range(N_IDX_HEADS):
            sc = sc + wix[:, hh:hh + 1] * jnp.maximum(_dot_nt(qix_ref[0, hh], kix), 0.0)
        if masked:
            sc = jnp.where(_chunk_visible(q0, k0, tq, l_keys), sc, -jnp.inf)
        bits = lax.bitcast_convert_type(sc, I32)
        key_scr[:, pl.ds(k0, TK)] = bits ^ ((bits >> 31) & 0x7FFFFFFF)

    def score_body(j, c):
        score_tile(j, False)
        return c

    lax.fori_loop(0, jd, score_body, 0)
    score_tile(jd, True)

    def count_rows(pred):
        def body(j, cnt):
            return cnt + pred(key_scr[:, pl.ds(_tile_start(j), TK)]).astype(I32)
        cnt = lax.fori_loop(0, n_vis, body, jnp.zeros((tq, TK), I32))
        return jnp.sum(cnt, axis=-1, keepdims=True)

    def bit_body(i, thr):
        cand = thr ^ lax.shift_left(jnp.int32(1), 31 - i)
        tot = count_rows(lambda key: key >= cand)
        return jnp.where(tot >= topk, cand, thr)

    thr = lax.fori_loop(0, 32, bit_body, jnp.full((tq, 1), INT_MIN, I32))

    need = (topk - count_rows(lambda key: key > thr)).astype(F32)
    upto = (lax.broadcasted_iota(I32, (TK, TK), 0)
            <= lax.broadcasted_iota(I32, (TK, TK), 1)).astype(BF16)

    def mask_body(j, seen):
        k0 = _tile_start(j)
        key = key_scr[:, pl.ds(k0, TK)]
        tie = key == thr
        rank = _dot(tie.astype(BF16), upto) + seen
        sel = (key > KEY_NEG_INF) & ((key > thr) | (tie & (rank <= need)))
        msk_scr[:, pl.ds(k0, TK)] = jnp.where(sel, 0.0, NEG)
        return seen + jnp.sum(tie.astype(F32), axis=-1, keepdims=True)

    lax.fori_loop(0, n_vis, mask_body, jnp.zeros((tq, 1), F32))

    for h in range(N_HEADS):
        o = _biased_flash(q_ref[0, h], k_ref, v_ref, h, bias_ref, N_HEADS + h, far_ref, q0, tq,
                          l_keys, extra=lambda k0: msk_scr[:, pl.ds(k0, TK)])
        o_ref[0, :, h * HEAD_DIM:(h + 1) * HEAD_DIM] = o


def _ds_attn(qix, wix, kix, q, k, v, bias, far, past, tq, l_keys, topk):
    b, _, t, _ = q.shape
    l_pad = k.shape[2]
    q_spec, kv_spec, o_spec = _attn_specs(N_HEADS, tq, l_pad)
    return pl.pallas_call(
        functools.partial(_ds_kernel, past=past, tq=tq, l_keys=l_keys, topk=topk),
        grid=(b, t // tq),
        in_specs=[pl.BlockSpec((1, N_IDX_HEADS, tq, IDX_DIM), lambda b, i: (b, 0, i, 0)),
                  pl.BlockSpec((1, tq, LANES), lambda b, i: (b, i, 0)),
                  _resident((1, l_pad, IDX_DIM), lambda b, i: (b, 0, 0)),
                  q_spec, kv_spec, kv_spec, _full(bias.shape), _full(far.shape)],
        out_specs=o_spec,
        out_shape=jax.ShapeDtypeStruct((b, t, GROUP_WIDTH), F32),
        scratch_shapes=[pltpu.VMEM((tq, l_pad), I32), pltpu.VMEM((tq, l_pad), F32)],
        compiler_params=_cparams(("parallel", "parallel")),
    )(qix, wix, kix, q, k, v, bias, far)


def _post_kernel(x_ref, o0_ref, o1_ref, o2_ref, o3_ref, wo_ref, g_ref, wg_ref, wu_ref, wd_ref,
                 cw_ref, cb_ref, pc_ref, xo_ref, st_ref, uext, *, tm):
    t = pl.program_id(1)

    @pl.when(t == 0)
    def _():
        uext[0:SUBLANES, :] = pc_ref[0]

    @pl.when(t > 0)
    def _():
        uext[0:SUBLANES, :] = uext[tm:tm + SUBLANES, :]

    x = x_ref[0]
    for gi, o_ref in enumerate((o0_ref, o1_ref, o2_ref, o3_ref)):
        x = x + _dot(o_ref[0].astype(BF16), wo_ref[gi * GROUP_WIDTH:(gi + 1) * GROUP_WIDTH, :])
    hn = (x * lax.rsqrt(jnp.mean(x * x, axis=-1, keepdims=True) + RMS_EPS) * g_ref[...]).astype(BF16)
    u = _dot(hn, wg_ref[...])
    uext[SUBLANES:SUBLANES + tm, :] = u
    conv = cb_ref[...]
    conv = conv + cw_ref[0:1, :] * uext[SUBLANES - 2:SUBLANES - 2 + tm, :]
    conv = conv + cw_ref[1:2, :] * uext[SUBLANES - 1:SUBLANES - 1 + tm, :]
    conv = conv + cw_ref[2:3, :] * u
    hidden = conv * (1.0 / (1.0 + jnp.exp(-conv))) * _dot(hn, wu_ref[...])
    xo_ref[0] = x + _dot(hidden.astype(BF16), wd_ref[...])
    st_ref[0] = uext[tm:tm + SUBLANES, :]


def _post(x, outs, wo, g, wg, wu, wd, cw, cb, pconv, tm):
    b, t, d = x.shape
    f = wg.shape[1]
    row = lambda w: pl.BlockSpec((1, tm, w), lambda bi, ti: (bi, ti, 0))
    const = lambda shape: _resident(shape, lambda bi, ti: (0,) * len(shape))
    return pl.pallas_call(
        functools.partial(_post_kernel, tm=tm),
        grid=(b, t // tm),
        in_specs=[row(d)] + [row(GROUP_WIDTH)] * 4 + [
            const(wo.shape), const(g.shape), const(wg.shape), const(wu.shape), const(wd.shape),
            const(cw.shape), const(cb.shape),
            pl.BlockSpec((1, SUBLANES, f), lambda bi, ti: (bi, 0, 0))],
        out_specs=[row(d), pl.BlockSpec((1, SUBLANES, f), lambda bi, ti: (bi, 0, 0))],
        out_shape=[jax.ShapeDtypeStruct((b, t, d), F32),
                   jax.ShapeDtypeStruct((b, SUBLANES, f), F32)],
        scratch_shapes=[pltpu.VMEM((tm + SUBLANES, f), F32)],
        compiler_params=_cparams(("parallel", "arbitrary")),
    )(x, *outs, wo, g, wg, wu, wd, cw, cb, pconv)


def _final_norm_kernel(x_ref, g_ref, o_ref):
    x = x_ref[...]
    o_ref[...] = x * lax.rsqrt(jnp.mean(x * x, axis=-1, keepdims=True) + RMS_EPS) * g_ref[...]


def _final_norm(x, g, tm):
    b, t, d = x.shape
    x2 = x.reshape(b * t, d)
    y = pl.pallas_call(
        _final_norm_kernel,
        grid=(b * t // tm,),
        in_specs=[pl.BlockSpec((tm, d), lambda i: (i, 0)), pl.BlockSpec((1, d), lambda i: (0, 0))],
        out_specs=pl.BlockSpec((tm, d), lambda i: (i, 0)),
        out_shape=jax.ShapeDtypeStruct(x2.shape, F32),
        compiler_params=_cparams(("parallel",)),
    )(x2, g.reshape(1, d))
    return y.reshape(b, t, d)


def _t5_bucket(rel):
    half = T5_BUCKETS // 2
    max_exact = half // 2
    base = jnp.where(rel > 0, half, 0)
    n = jnp.abs(rel)
    nf = jnp.maximum(n, 1).astype(F32)
    large = max_exact + (jnp.log(nf / max_exact) / math.log(T5_MAX_DIST / max_exact)
                         * (half - max_exact)).astype(I32)
    large = jnp.minimum(large, half - 1)
    return base + jnp.where(n < max_exact, n, large)


def _bias_tables(t5_table):
    i = jnp.arange(TK, dtype=I32)
    rel = jnp.stack([i[None, :] - i[:, None], i[None, :] - i[:, None] - TK])
    tab = t5_table.astype(F32).T
    tiles = jnp.moveaxis(tab[:, _t5_bucket(rel)], 0, 1)
    far = jnp.broadcast_to(tab[:, _t5_bucket(jnp.int32(-2 * TK))][:, None], (tab.shape[0], LANES))
    return tiles, far


def _pack_w_in(w_in):
    d = w_in.shape[0]
    fx0 = COL_SLAB + IDX_DIM + N_IDX_HEADS
    pad = jnp.zeros((d, LANES - SLAB_GATE - N_HEADS), w_in.dtype)
    return jnp.concatenate([w_in[:, :fx0], w_in[:, fx0 + 3 * GROUP_WIDTH:], pad,
                            w_in[:, fx0:fx0 + 3 * GROUP_WIDTH]], axis=1).astype(BF16)


def _heads(a, n_heads=N_HEADS):
    b, t, _ = a.shape
    return jnp.transpose(a.reshape(b, t, n_heads, HEAD_DIM), (0, 2, 1, 3)).astype(BF16)


def _with_past(past, new, l_pad):
    b, t = new.shape[:2]
    new = new.reshape(b, t, -1)
    if past is not None:
        new = jnp.concatenate([past.reshape(b, past.shape[1], -1).astype(F32), new], axis=1)
    return jnp.pad(new, ((0, 0), (0, l_pad - new.shape[1]), (0, 0)))


def _layer(x, past, lw, lam_init, bias_tiles, bias_far, tq, tm_proj, tm_post):
    (g_mix, w_in_p, b_forget, w_out, lam_q1, lam_k1, lam_q2, lam_k2, diff_subln,
     g_ffn, w_gate, w_up, w_down, conv_w, conv_b) = lw
    b, t, d = x.shape
    p_len = 0 if past is None else past[0].shape[1]
    l_keys = p_len + t
    l_pad = -(-l_keys // TK) * TK
    topk = min(TOPK_MAX, l_keys // 4)
    assert p_len % TK == 0 and (tq == TK and t % TK == 0 or tq == t and t <= TK)
    (pk_sb, pv_sb, pk_df, pv_df, pk_ds, pv_ds, pk_ix, pk_fx, pv_fx, plogf, pconv) = (
        past if past is not None else (None,) * 11)

    bf = jnp.zeros((1, LANES), F32).at[0, SLAB_GATE:SLAB_GATE + N_HEADS].set(b_forget)
    y, lf = _proj(x.reshape(b * t, d), g_mix.reshape(1, d), w_in_p, bf, tm_proj)
    y = y.reshape(b, t, PROJ_PAD)
    grp = lambda i: y[:, :, i * GROUP_WIDTH:(i + 1) * GROUP_WIDTH]
    q_sb, k_sb, v_sb, q_df, k_df, v_df, q_ds, k_ds, v_ds = [grp(i) for i in range(9)]
    q_ix = y[:, :, COL_QIX:COL_SLAB]
    slab = y[:, :, COL_SLAB:COL_FX]
    k_ix = slab[:, :, :IDX_DIM]
    q_fx, k_fx, v_fx = [y[:, :, COL_FX + i * GROUP_WIDTH:COL_FX + (i + 1) * GROUP_WIDTH] for i in range(3)]
    logf = lf.reshape(b, t, LANES)[:, :, SLAB_GATE:SLAB_GATE + N_HEADS]

    keys = lambda pst, new: _heads(_with_past(pst, new, l_pad))
    scale = HEAD_DIM ** -0.5

    o_sb = _sb_attn(_heads(q_sb * scale), keys(pk_sb, k_sb), keys(pv_sb, v_sb), p_len, tq)

    half = (jnp.arange(HEAD_DIM) < DIFF_DH)
    q_df4 = q_df.reshape(b, t, N_HEADS, 1, HEAD_DIM) * (DIFF_DH ** -0.5)
    q_df2 = jnp.where(jnp.stack([half, ~half])[None, None, None], q_df4, 0.0)
    lam_rows = jnp.zeros((SUBLANES, LANES), F32)
    for r, vec in enumerate((lam_q1, lam_k1, lam_q2, lam_k2)):
        lam_rows = lam_rows.at[r, :DIFF_DH].set(vec.astype(F32))
    lam_rows = lam_rows.at[4, :].set(lam_init)
    o_df = _df_attn(_heads(q_df2.reshape(b, t, 2 * GROUP_WIDTH), 2 * N_HEADS), keys(pk_df, k_df),
                    keys(pv_df, v_df), bias_tiles[:, :N_HEADS], bias_far, lam_rows,
                    diff_subln.reshape(1, HEAD_DIM).astype(F32), p_len, tq, l_keys)

    kix_all = _with_past(pk_ix, k_ix, l_pad).astype(BF16)
    wix = jnp.pad(slab[:, :, SLAB_WIX:SLAB_WIX + N_IDX_HEADS], ((0, 0), (0, 0), (0, LANES - N_IDX_HEADS)))
    o_ds = _ds_attn(_heads(q_ix * (IDX_DIM ** -0.5), N_IDX_HEADS), wix, kix_all, _heads(q_ds * scale),
                    keys(pk_ds, k_ds), keys(pv_ds, v_ds), bias_tiles, bias_far, p_len, tq, l_keys, topk)

    logf_all = _with_past(plogf, logf, l_pad)
    c_all = _cumsum_seq(logf_all)
    cq = jnp.pad(jnp.transpose(c_all[:, :, p_len:p_len + t], (0, 2, 1)), ((0, 0), (0, 0), (0, LANES - N_HEADS)))
    o_fx = _fx_attn(_heads(q_fx * scale), keys(pk_fx, k_fx), keys(pv_fx, v_fx), cq, c_all, p_len, tq)

    f = w_gate.shape[1]
    if pconv is None:
        pc = jnp.zeros((b, SUBLANES, f), F32)
    else:
        pc = jnp.pad(pconv.astype(F32), ((0, 0), (SUBLANES - (CONV_W - 1), 0), (0, 0)))
    cw = jnp.pad(conv_w.astype(F32), ((0, SUBLANES - CONV_W), (0, 0)))
    x_new, st = _post(x, (o_sb, o_df, o_ds, o_fx), w_out, g_ffn.reshape(1, d), w_gate, w_up, w_down,
                      cw, conv_b.reshape(1, f).astype(F32), pc, tm_post)

    hd = lambda a: a.reshape(b, t, N_HEADS, HEAD_DIM)
    rows = (hd(k_sb), hd(v_sb), k_df.reshape(b, t, N_HEADS, 2, DIFF_DH), hd(v_df), hd(k_ds), hd(v_ds),
            k_ix, hd(k_fx), hd(v_fx), logf, st[:, SUBLANES - (CONV_W - 1):])
    return x_new, rows


def _row_tile(n, cap):
    tm = min(n, cap)
    assert n % tm == 0
    return tm


def kernel(x_prompt, x_sample, cache_sb_k, cache_sb_v, cache_diff_k, cache_diff_v, cache_dsa_k, cache_dsa_v, cache_dsa_kidx, cache_fox_k, cache_fox_v, cache_fox_logf, state_ffn_conv, g_mix, w_in, b_forget, w_out, lam_q1, lam_k1, lam_q2, lam_k2, diff_subln, g_ffn, w_gate, w_up, w_down, conv_w, conv_b, t5_table, g_final):
    depth = w_in.shape[0]
    bias_tiles, bias_far = _bias_tables(t5_table)
    bp, tp, _ = x_prompt.shape
    bs, ts, _ = x_sample.shape
    tq_p, tq_s = min(tp, TK), min(ts, TK)
    xp, xs = x_prompt, x_sample
    p_rows, s_rows = [], []
    for l in range(depth):
        lw = (g_mix[l], _pack_w_in(w_in[l]), b_forget[l], w_out[l].astype(BF16), lam_q1[l], lam_k1[l],
              lam_q2[l], lam_k2[l], diff_subln[l], g_ffn[l], w_gate[l].astype(BF16), w_up[l].astype(BF16),
              w_down[l].astype(BF16), conv_w[l], conv_b[l])
        lam_init = 0.8 - 0.6 * math.exp(-0.3 * l)
        xp, rp = _layer(xp, None, lw, lam_init, bias_tiles, bias_far, tq_p,
                        _row_tile(bp * tp, 512), _row_tile(tp, 256))
        s_past = (cache_sb_k[l], cache_sb_v[l], cache_diff_k[l], cache_diff_v[l], cache_dsa_k[l],
                  cache_dsa_v[l], cache_dsa_kidx[l], cache_fox_k[l], cache_fox_v[l], cache_fox_logf[l],
                  state_ffn_conv[l])
        xs, rs = _layer(xs, s_past, lw, lam_init, bias_tiles, bias_far, tq_s,
                        _row_tile(bs * ts, 512), _row_tile(ts, 256))
        p_rows.append(rp)
        s_rows.append(rs)
    y_prompt = _final_norm(xp, g_final, _row_tile(bp * tp, 512))
    y_sample = _final_norm(xs, g_final, _row_tile(bs * ts, 512))
    stack = lambda rows, i: jnp.stack([r[i] for r in rows], axis=0)
    return ((y_prompt, y_sample) + tuple(stack(p_rows, i) for i in range(11))
            + tuple(stack(s_rows, i) for i in range(11)))
```

```python
import functools
import math

import jax
import jax.numpy as jnp
from jax import lax
from jax.experimental import pallas as pl
from jax.experimental.pallas import tpu as pltpu

F32 = jnp.float32
BF16 = jnp.bfloat16
I32 = jnp.int32

N_HEADS = 4
HEAD_DIM = 64
GROUP_WIDTH = N_HEADS * HEAD_DIM
DIFF_DH = HEAD_DIM // 2
N_IDX_HEADS = 8
IDX_DIM = 64
CHUNK = 64
TOPK_MAX = 256
T5_BUCKETS = 32
T5_MAX_DIST = 128
CONV_W = 3
RMS_EPS = 1e-6
SUBLN_EPS = 1e-5

LANES = 128
SUBLANES = 8
SB_KB = 256
KB = 512
FRONT = 256
TQ_MAX = 256
N_EXTRA = 3
COL_FLAG = HEAD_DIM + N_EXTRA
V_ROWS = 80
NEG = -1e30
SB_DEAD = -110.0
INT_MIN = -2 ** 31
KEY_NEG_INF = -2139095041
LOG2E = math.log2(math.e)
VMEM_LIMIT = 56 * 1024 * 1024

COL_QIX = 9 * GROUP_WIDTH
COL_SLAB = COL_QIX + N_IDX_HEADS * IDX_DIM
SLAB_WIX = IDX_DIM
SLAB_GATE = IDX_DIM + N_IDX_HEADS
COL_FX = COL_SLAB + LANES
PROJ_PAD = COL_FX + 3 * GROUP_WIDTH


def _cparams(sem):
    return pltpu.CompilerParams(dimension_semantics=sem, vmem_limit_bytes=VMEM_LIMIT)


def _resident(shape, index_map):
    return pl.BlockSpec(shape, index_map, pipeline_mode=pl.Buffered(1))


def _full(shape):
    return _resident(shape, lambda b, i: (0,) * len(shape))


def _log_sigmoid(z):
    return jnp.minimum(z, 0.0) - jnp.log(1.0 + jnp.exp(-jnp.abs(z)))


def _dot(a, b):
    return jnp.dot(a, b, preferred_element_type=F32)


def _proj_kernel(x_ref, g_ref, w_ref, bf_ref, y_ref, lf_ref):
    x = x_ref[...]
    hn = x * lax.rsqrt(jnp.mean(x * x, axis=-1, keepdims=True) + RMS_EPS) * g_ref[...]
    y = _dot(hn.astype(BF16), w_ref[...])
    y_ref[...] = y
    lf_ref[...] = _log_sigmoid(y[:, COL_SLAB:COL_SLAB + LANES] + bf_ref[...])


def _proj(x2, g, w, bf, tm):
    n, d = x2.shape
    return pl.pallas_call(
        _proj_kernel,
        grid=(n // tm,),
        in_specs=[pl.BlockSpec((tm, d), lambda i: (i, 0)),
                  pl.BlockSpec((1, d), lambda i: (0, 0)),
                  _resident((d, PROJ_PAD), lambda i: (0, 0)),
                  pl.BlockSpec((1, LANES), lambda i: (0, 0))],
        out_specs=[pl.BlockSpec((tm, PROJ_PAD), lambda i: (i, 0)),
                   pl.BlockSpec((tm, LANES), lambda i: (i, 0))],
        out_shape=[jax.ShapeDtypeStruct((n, PROJ_PAD), F32),
                   jax.ShapeDtypeStruct((n, LANES), F32)],
        compiler_params=_cparams(("parallel",)),
        name="proj",
    )(x2, g, w, bf)


def _cumsum_kernel(x_ref, o_ref):
    nch, rows, _ = x_ref.shape
    lane = lax.broadcasted_iota(I32, (rows, LANES), 1)

    def body(c, carry):
        x = x_ref[c]
        d = 1
        while d < LANES:
            x = x + jnp.where(lane >= d, pltpu.roll(x, d, 1), 0.0)
            d *= 2
        x = x + carry
        o_ref[c] = x
        return jnp.broadcast_to(x[:, LANES - 1:LANES], (rows, LANES))

    lax.fori_loop(0, nch, body, jnp.zeros((rows, LANES), F32))


def _cumsum_seq(logf_all):
    b, l, h = logf_all.shape
    rows = b * h
    rows_pad = -(-rows // SUBLANES) * SUBLANES
    x = jnp.transpose(logf_all, (0, 2, 1)).reshape(rows, l // LANES, LANES)
    x = jnp.pad(jnp.transpose(x, (1, 0, 2)), ((0, 0), (0, rows_pad - rows), (0, 0)))
    c = pl.pallas_call(
        _cumsum_kernel,
        out_shape=jax.ShapeDtypeStruct(x.shape, F32),
        compiler_params=pltpu.CompilerParams(vmem_limit_bytes=VMEM_LIMIT),
        name="cumsum",
    )(x)
    return jnp.transpose(c[:, :rows], (1, 0, 2)).reshape(b, h, l)


def _sb_kernel(qt_ref, k_ref, vt_ref, later_ref, o_ref, run_scr, acc_scr, *, past, tq):
    q0 = pl.multiple_of(past + pl.program_id(1) * tq, SB_KB)
    run_scr[...] = jnp.zeros(run_scr.shape, F32)
    acc_scr[...] = jnp.zeros(acc_scr.shape, F32)
    later = later_ref[...]

    def block(start, strict):
        logits = [_dot(k_ref[0, h, pl.ds(start, SB_KB), :], qt_ref[0, h]) for h in range(N_HEADS)]
        for h in range(N_HEADS):
            z = logits[h]
            ls = _log_sigmoid(z)
            lk = ls - z
            if strict is not None:
                lk = jnp.where(strict, lk, 0.0)
            hi = lk.astype(BF16)
            rem = lk - hi.astype(F32)
            mid = rem.astype(BF16)
            lo = (rem - mid.astype(F32)).astype(BF16)
            between = _dot(later, hi) + _dot(later, mid) + _dot(later, lo) + run_scr[h]
            a = jnp.exp(ls + between)
            if strict is not None:
                a = jnp.where(strict, a, 0.0)
            acc_scr[h] = acc_scr[h] + _dot(vt_ref[0, h, :, pl.ds(start, SB_KB)], a.astype(BF16))
            run_scr[h] = run_scr[h] + jnp.sum(lk, axis=0, keepdims=True)

    block(q0, lax.broadcasted_iota(I32, (SB_KB, tq), 0) < lax.broadcasted_iota(I32, (SB_KB, tq), 1))

    def cond(carry):
        return (carry[0] >= 0) & (carry[1] > SB_DEAD)

    def body(carry):
        block(pl.multiple_of(carry[0], SB_KB), None)
        return carry[0] - SB_KB, jnp.max(run_scr[...])

    lax.while_loop(cond, body, (q0 - SB_KB, jnp.max(run_scr[...])))
    for h in range(N_HEADS):
        o_ref[0, h * HEAD_DIM:(h + 1) * HEAD_DIM, :] = acc_scr[h]


def _sb_attn(qt, k, vt, later, past, tq):
    b, _, _, t = qt.shape
    l_sb = k.shape[2]
    return pl.pallas_call(
        functools.partial(_sb_kernel, past=past, tq=tq),
        grid=(b, t // tq),
        in_specs=[pl.BlockSpec((1, N_HEADS, HEAD_DIM, tq), lambda b, i: (b, 0, 0, i)),
                  _resident((1, N_HEADS, l_sb, HEAD_DIM), lambda b, i: (b, 0, 0, 0)),
                  _resident((1, N_HEADS, HEAD_DIM, l_sb), lambda b, i: (b, 0, 0, 0)),
                  _full(later.shape)],
        out_specs=pl.BlockSpec((1, GROUP_WIDTH, tq), lambda b, i: (b, 0, i)),
        out_shape=jax.ShapeDtypeStruct((b, GROUP_WIDTH, t), F32),
        scratch_shapes=[pltpu.VMEM((N_HEADS, 1, tq), F32), pltpu.VMEM((N_HEADS, HEAD_DIM, tq), F32)],
        compiler_params=_cparams(("parallel", "parallel")),
        name="stick_breaking",
    )(qt, k, vt, later)


def _n_far_blocks(q0):
    return jnp.maximum(q0 - FRONT + KB - 1, 0) // KB


def _far_start(q0, j):
    return pl.multiple_of(q0 - KB * (j + 1), FRONT)


def _flash_tile(n_maps, head_of, qa_ref, ka_ref, va_ref, m_scr, acc_scr, q0, near_add, far_add):
    tq = m_scr.shape[-1]
    m_scr[...] = jnp.full(m_scr.shape, NEG, F32)
    acc_scr[...] = jnp.zeros(acc_scr.shape, F32)

    def block(start, add):
        scores = []
        for mp in range(n_maps):
            s = _dot(ka_ref[0, head_of(mp), pl.ds(start, KB), :], qa_ref[0, mp])
            scores.append(s if add is None else s + add(mp))
        for mp in range(n_maps):
            s = scores[mp]
            m_old = m_scr[mp]
            m_new = jnp.maximum(m_old, jnp.max(s, axis=0, keepdims=True))
            p = jnp.exp2(s - m_new).astype(BF16)
            acc_scr[mp] = (acc_scr[mp] * jnp.exp2(m_old - m_new)
                           + _dot(va_ref[0, head_of(mp), :, pl.ds(start, KB)], p))
            m_scr[mp] = m_new

    block(pl.multiple_of(q0, FRONT), near_add)

    def far(j, c):
        start = _far_start(q0, j)
        block(start, None if far_add is None else (lambda mp: far_add(start)))
        return c

    lax.fori_loop(0, _n_far_blocks(q0), far, 0)


def _normalized(acc_scr, mp):
    return acc_scr[mp, 0:HEAD_DIM, :] / acc_scr[mp, HEAD_DIM:HEAD_DIM + 1, :]


def _flash_specs(n_maps, tq, lp):
    qa_spec = pl.BlockSpec((1, n_maps, LANES, tq), lambda b, i: (b, 0, 0, i))
    ka_spec = _resident((1, N_HEADS, lp, LANES), lambda b, i: (b, 0, 0, 0))
    va_spec = _resident((1, N_HEADS, V_ROWS, lp), lambda b, i: (b, 0, 0, 0))
    o_spec = pl.BlockSpec((1, GROUP_WIDTH, tq), lambda b, i: (b, 0, i))
    scratch = [pltpu.VMEM((n_maps, 1, tq), F32), pltpu.VMEM((n_maps, V_ROWS, tq), F32)]
    return qa_spec, ka_spec, va_spec, o_spec, scratch


def _fx_kernel(qa_ref, ka_ref, va_ref, causal_ref, o_ref, m_scr, acc_scr, *, past, tq):
    q0 = past + pl.program_id(1) * tq
    _flash_tile(N_HEADS, lambda mp: mp, qa_ref, ka_ref, va_ref, m_scr, acc_scr, q0,
                lambda mp: causal_ref[...], None)
    for h in range(N_HEADS):
        o_ref[0, h * HEAD_DIM:(h + 1) * HEAD_DIM, :] = _normalized(acc_scr, h)


def _fx_attn(qa, ka, va, causal, past, tq):
    b, _, _, t = qa.shape
    qa_spec, ka_spec, va_spec, o_spec, scratch = _flash_specs(N_HEADS, tq, ka.shape[2])
    return pl.pallas_call(
        functools.partial(_fx_kernel, past=past, tq=tq),
        grid=(b, t // tq),
        in_specs=[qa_spec, ka_spec, va_spec, _full(causal.shape)],
        out_specs=o_spec,
        out_shape=jax.ShapeDtypeStruct((b, GROUP_WIDTH, t), F32),
        scratch_shapes=scratch,
        compiler_params=_cparams(("parallel", "parallel")),
        name="forgetting",
    )(qa, ka, va, causal)


def _df_kernel(qa_ref, ka_ref, va_ref, near_ref, lam_ref, g_ref, o_ref, m_scr, acc_scr, *, past, tq):
    q0 = past + pl.program_id(1) * tq
    _flash_tile(2 * N_HEADS, lambda mp: mp // 2, qa_ref, ka_ref, va_ref, m_scr, acc_scr, q0,
                lambda mp: near_ref[mp // 2], None)
    lam_init = lam_ref[4:5, 0:1]
    lam = (jnp.exp(jnp.sum(lam_ref[0:1, :] * lam_ref[1:2, :], axis=-1, keepdims=True))
           - jnp.exp(jnp.sum(lam_ref[2:3, :] * lam_ref[3:4, :], axis=-1, keepdims=True)) + lam_init)
    for h in range(N_HEADS):
        o = _normalized(acc_scr, 2 * h) - lam * _normalized(acc_scr, 2 * h + 1)
        o = o * lax.rsqrt(jnp.mean(o * o, axis=0, keepdims=True) + SUBLN_EPS) * g_ref[...]
        o_ref[0, h * HEAD_DIM:(h + 1) * HEAD_DIM, :] = o * (1.0 - lam_init)


def _df_attn(qa, ka, va, near, lam_rows, g_cols, past, tq):
    b, _, _, t = qa.shape
    qa_spec, ka_spec, va_spec, o_spec, scratch = _flash_specs(2 * N_HEADS, tq, ka.shape[2])
    return pl.pallas_call(
        functools.partial(_df_kernel, past=past, tq=tq),
        grid=(b, t // tq),
        in_specs=[qa_spec, ka_spec, va_spec, _full(near.shape), _full(lam_rows.shape), _full(g_cols.shape)],
        out_specs=o_spec,
        out_shape=jax.ShapeDtypeStruct((b, GROUP_WIDTH, t), F32),
        scratch_shapes=scratch,
        compiler_params=_cparams(("parallel", "parallel")),
        name="differential",
    )(qa, ka, va, near, lam_rows, g_cols)


def _ds_kernel(qix_ref, wix_ref, kix_ref, qa_ref, ka_ref, va_ref, near_ref, upto_ref, o_ref,
               key_scr, msk_scr, m_scr, acc_scr, *, past, tq, l_keys, topk):
    q0 = past + pl.program_id(1) * tq
    near0 = pl.multiple_of(q0, FRONT)
    n_far = _n_far_blocks(q0)
    wix = wix_ref[0] * (N_IDX_HEADS ** -0.5)

    def score_block(start, near):
        kix = kix_ref[0, pl.ds(start, KB), :]
        sc = jnp.zeros((KB, tq), F32)
        for hh in range(N_IDX_HEADS):
            sc = sc + wix[hh:hh + 1, :] * jnp.maximum(_dot(kix, qix_ref[0, hh]), 0.0)
        kidx = start + lax.broadcasted_iota(I32, (KB, tq), 0)
        ok = (kidx >= FRONT) & (kidx < FRONT + l_keys)
        if near:
            ok = ok & (near_ref[0] > 0.5 * NEG)
        bits = lax.bitcast_convert_type(jnp.where(ok, sc, -jnp.inf), I32)
        key_scr[pl.ds(start, KB), :] = bits ^ ((bits >> 31) & 0x7FFFFFFF)

    score_block(near0, True)

    def score_far(j, c):
        score_block(_far_start(q0, j), False)
        return c

    lax.fori_loop(0, n_far, score_far, 0)

    def count(pred):
        def blk(start):
            return jnp.sum(pred(key_scr[pl.ds(start, KB), :]).astype(I32), axis=0, keepdims=True)
        return lax.fori_loop(0, n_far, lambda j, c: c + blk(_far_start(q0, j)), blk(near0))

    def bit_body(i, thr):
        cand = thr ^ lax.shift_left(jnp.int32(1), 31 - i)
        return jnp.where(count(lambda key: key >= cand) >= topk, cand, thr)

    thr = lax.fori_loop(0, 32, bit_body, jnp.full((1, tq), INT_MIN, I32))

    need = (topk - count(lambda key: key > thr)).astype(F32)

    def mask_block(start, seen):
        key = key_scr[pl.ds(start, KB), :]
        tie = key == thr
        rank = _dot(upto_ref[...], tie.astype(BF16)) + seen
        sel = (key > KEY_NEG_INF) & ((key > thr) | (tie & (rank <= need)))
        msk_scr[pl.ds(start, KB), :] = jnp.where(sel, 0.0, NEG)
        return seen + jnp.sum(tie.astype(F32), axis=0, keepdims=True)

    seen = lax.fori_loop(0, n_far, lambda i, c: mask_block(_far_start(q0, n_far - 1 - i), c),
                         jnp.zeros((1, tq), F32))
    mask_block(near0, seen)

    _flash_tile(N_HEADS, lambda mp: mp, qa_ref, ka_ref, va_ref, m_scr, acc_scr, q0,
                lambda mp: near_ref[mp] + msk_scr[pl.ds(near0, KB), :],
                lambda start: msk_scr[pl.ds(start, KB), :])
    for h in range(N_HEADS):
        o_ref[0, h * HEAD_DIM:(h + 1) * HEAD_DIM, :] = _normalized(acc_scr, h)


def _ds_attn(qix, wix, kix, qa, ka, va, near, upto, past, tq, l_keys, topk):
    b, _, _, t = qa.shape
    lp = ka.shape[2]
    qa_spec, ka_spec, va_spec, o_spec, scratch = _flash_specs(N_HEADS, tq, lp)
    return pl.pallas_call(
        functools.partial(_ds_kernel, past=past, tq=tq, l_keys=l_keys, topk=topk),
        grid=(b, t // tq),
        in_specs=[pl.BlockSpec((1, N_IDX_HEADS, IDX_DIM, tq), lambda b, i: (b, 0, 0, i)),
                  pl.BlockSpec((1, N_IDX_HEADS, tq), lambda b, i: (b, 0, i)),
                  _resident((1, lp, IDX_DIM), lambda b, i: (b, 0, 0)),
                  qa_spec, ka_spec, va_spec, _full(near.shape), _full(upto.shape)],
        out_specs=o_spec,
        out_shape=jax.ShapeDtypeStruct((b, GROUP_WIDTH, t), F32),
        scratch_shapes=[pltpu.VMEM((lp, tq), I32), pltpu.VMEM((lp, tq), F32)] + scratch,
        compiler_params=_cparams(("parallel", "parallel")),
        name="indexed_sparse",
    )(qix, wix, kix, qa, ka, va, near, upto)


def _post_kernel(x_ref, o0_ref, o1_ref, o2_ref, o3_ref, wo_ref, g_ref, wg_ref, wu_ref, wd_ref,
                 cw_ref, cb_ref, pc_ref, xo_ref, st_ref, uext, *, tm):
    t = pl.program_id(1)

    @pl.when(t == 0)
    def _():
        uext[0:SUBLANES, :] = pc_ref[0]

    @pl.when(t > 0)
    def _():
        uext[0:SUBLANES, :] = uext[tm:tm + SUBLANES, :]

    x = x_ref[0]
    for gi, o_ref in enumerate((o0_ref, o1_ref, o2_ref, o3_ref)):
        x = x + _dot(o_ref[0].astype(BF16), wo_ref[gi * GROUP_WIDTH:(gi + 1) * GROUP_WIDTH, :])
    hn = (x * lax.rsqrt(jnp.mean(x * x, axis=-1, keepdims=True) + RMS_EPS) * g_ref[...]).astype(BF16)
    u = _dot(hn, wg_ref[...])
    uext[SUBLANES:SUBLANES + tm, :] = u
    conv = cb_ref[...]
    conv = conv + cw_ref[0:1, :] * uext[SUBLANES - 2:SUBLANES - 2 + tm, :]
    conv = conv + cw_ref[1:2, :] * uext[SUBLANES - 1:SUBLANES - 1 + tm, :]
    conv = conv + cw_ref[2:3, :] * u
    hidden = conv * (1.0 / (1.0 + jnp.exp(-conv))) * _dot(hn, wu_ref[...])
    xo_ref[0] = x + _dot(hidden.astype(BF16), wd_ref[...])
    st_ref[0] = uext[tm:tm + SUBLANES, :]


def _post(x, outs, wo, g, wg, wu, wd, cw, cb, pconv, tm):
    b, t, d = x.shape
    f = wg.shape[1]
    row = lambda w: pl.BlockSpec((1, tm, w), lambda bi, ti: (bi, ti, 0))
    const = lambda shape: _resident(shape, lambda bi, ti: (0,) * len(shape))
    return pl.pallas_call(
        functools.partial(_post_kernel, tm=tm),
        grid=(b, t // tm),
        in_specs=[row(d)] + [row(GROUP_WIDTH)] * 4 + [
            const(wo.shape), const(g.shape), const(wg.shape), const(wu.shape), const(wd.shape),
            const(cw.shape), const(cb.shape),
            pl.BlockSpec((1, SUBLANES, f), lambda bi, ti: (bi, 0, 0))],
        out_specs=[row(d), pl.BlockSpec((1, SUBLANES, f), lambda bi, ti: (bi, 0, 0))],
        out_shape=[jax.ShapeDtypeStruct((b, t, d), F32),
                   jax.ShapeDtypeStruct((b, SUBLANES, f), F32)],
        scratch_shapes=[pltpu.VMEM((tm + SUBLANES, f), F32)],
        compiler_params=_cparams(("parallel", "arbitrary")),
        name="out_proj_mlp",
    )(x, *outs, wo, g, wg, wu, wd, cw, cb, pconv)


def _final_norm_kernel(x_ref, g_ref, o_ref):
    x = x_ref[...]
    o_ref[...] = x * lax.rsqrt(jnp.mean(x * x, axis=-1, keepdims=True) + RMS_EPS) * g_ref[...]


def _final_norm(x, g, tm):
    b, t, d = x.shape
    x2 = x.reshape(b * t, d)
    y = pl.pallas_call(
        _final_norm_kernel,
        grid=(b * t // tm,),
        in_specs=[pl.BlockSpec((tm, d), lambda i: (i, 0)), pl.BlockSpec((1, d), lambda i: (0, 0))],
        out_specs=pl.BlockSpec((tm, d), lambda i: (i, 0)),
        out_shape=jax.ShapeDtypeStruct(x2.shape, F32),
        compiler_params=_cparams(("parallel",)),
        name="final_norm",
    )(x2, g.reshape(1, d))
    return y.reshape(b, t, d)


def _t5_bucket(rel):
    half = T5_BUCKETS // 2
    max_exact = half // 2
    base = jnp.where(rel > 0, half, 0)
    n = jnp.abs(rel)
    nf = jnp.maximum(n, 1).astype(F32)
    large = max_exact + (jnp.log(nf / max_exact) / math.log(T5_MAX_DIST / max_exact)
                         * (half - max_exact)).astype(I32)
    large = jnp.minimum(large, half - 1)
    return base + jnp.where(n < max_exact, n, large)


def _bf16_part(x):
    bits = lax.bitcast_convert_type(x, jnp.uint32) & jnp.uint32(0xFFFF0000)
    return lax.bitcast_convert_type(bits, F32)


def _split3(x):
    hi = _bf16_part(x)
    mid = _bf16_part(x - hi)
    lo = _bf16_part(x - hi - mid)
    return jnp.stack([hi, mid, lo], axis=-1)


def _near_tables(t5_table, tq):
    kk = jnp.arange(KB, dtype=I32)[:, None] - FRONT
    qq = jnp.arange(tq, dtype=I32)[None, :]
    tab = t5_table.astype(F32).T
    far = tab[:, _t5_bucket(jnp.int32(-T5_MAX_DIST))]
    bias = tab[:, _t5_bucket(kk - qq)]
    chunk_ok = (kk // CHUNK) <= (qq // CHUNK)
    near = jnp.where(chunk_ok[None], (bias - far[:, None, None]) * LOG2E, NEG)
    causal = jnp.where(kk <= qq, 0.0, NEG).astype(F32)
    return near, far * LOG2E, causal


def _pack_w_in(w_in):
    d = w_in.shape[0]
    fx0 = COL_SLAB + IDX_DIM + N_IDX_HEADS
    pad = jnp.zeros((d, LANES - SLAB_GATE - N_HEADS), w_in.dtype)
    return jnp.concatenate([w_in[:, :fx0], w_in[:, fx0 + 3 * GROUP_WIDTH:], pad,
                            w_in[:, fx0:fx0 + 3 * GROUP_WIDTH]], axis=1).astype(BF16)


def _with_past(past, new, l_pad):
    b, t = new.shape[:2]
    new = new.reshape(b, t, -1)
    if past is not None:
        new = jnp.concatenate([past.reshape(b, past.shape[1], -1).astype(F32), new], axis=1)
    return jnp.pad(new, ((0, 0), (0, l_pad - new.shape[1]), (0, 0)))


def _aug_queries(q, rows):
    b, t, m, _ = q.shape
    extra = jnp.zeros((m, LANES - HEAD_DIM), F32).at[:, :N_EXTRA].set(rows).at[:, N_EXTRA].set(1.0)
    extra = jnp.broadcast_to(extra[None, :, :, None], (b, m, LANES - HEAD_DIM, t))
    return jnp.concatenate([jnp.transpose(q, (0, 2, 3, 1)), extra], axis=2).astype(BF16)


def _aug_keys(k, cols, lp):
    b, l, h, _ = k.shape
    body = jnp.concatenate([k, cols, jnp.zeros((b, l, h, LANES - COL_FLAG), F32)], axis=-1)
    pad_row = jnp.zeros((LANES,), F32).at[COL_FLAG].set(NEG)
    pad = lambda n: jnp.broadcast_to(pad_row, (b, n, h, LANES))
    full = jnp.concatenate([pad(FRONT), body, pad(lp - FRONT - l)], axis=1)
    return jnp.transpose(full, (0, 2, 1, 3)).astype(BF16)


def _aug_values(v, lp):
    b, l, h, _ = v.shape
    vt = jnp.transpose(v, (0, 2, 3, 1))
    body = jnp.concatenate([vt, jnp.ones((b, h, 1, l), F32),
                            jnp.zeros((b, h, V_ROWS - HEAD_DIM - 1, l), F32)], axis=2)
    return jnp.pad(body, ((0, 0), (0, 0), (0, 0), (FRONT, lp - FRONT - l))).astype(BF16)


def _layer(x, past, lw, lam_init, t5_table, tm_proj, tm_post):
    (g_mix, w_in_p, b_forget, w_out, lam_q1, lam_k1, lam_q2, lam_k2, diff_subln,
     g_ffn, w_gate, w_up, w_down, conv_w, conv_b) = lw
    b, t, d = x.shape
    p_len = 0 if past is None else past[0].shape[1]
    l_keys = p_len + t
    l_pad = -(-l_keys // LANES) * LANES
    topk = min(TOPK_MAX, l_keys // 4)
    tq = min(t, TQ_MAX)
    assert p_len % FRONT == 0 and (t % TQ_MAX == 0 or t <= LANES) and FRONT == SB_KB == TQ_MAX
    lp = -(-max(FRONT + l_keys, p_len + t - tq + KB) // LANES) * LANES
    l_sb = p_len + t - tq + SB_KB
    (pk_sb, pv_sb, pk_df, pv_df, pk_ds, pv_ds, pk_ix, pk_fx, pv_fx, plogf, pconv) = (
        past if past is not None else (None,) * 11)

    bf = jnp.zeros((1, LANES), F32).at[0, SLAB_GATE:SLAB_GATE + N_HEADS].set(b_forget)
    y, lf = _proj(x.reshape(b * t, d), g_mix.reshape(1, d), w_in_p, bf, tm_proj)
    y = y.reshape(b, t, PROJ_PAD)
    grp = lambda i: y[:, :, i * GROUP_WIDTH:(i + 1) * GROUP_WIDTH]
    q_sb, k_sb, v_sb, q_df, k_df, v_df, q_ds, k_ds, v_ds = [grp(i) for i in range(9)]
    q_ix = y[:, :, COL_QIX:COL_SLAB]
    slab = y[:, :, COL_SLAB:COL_FX]
    k_ix = slab[:, :, :IDX_DIM]
    w_ix = slab[:, :, SLAB_WIX:SLAB_WIX + N_IDX_HEADS]
    q_fx, k_fx, v_fx = [y[:, :, COL_FX + i * GROUP_WIDTH:COL_FX + (i + 1) * GROUP_WIDTH] for i in range(3)]
    logf = lf.reshape(b, t, LANES)[:, :, SLAB_GATE:SLAB_GATE + N_HEADS]

    scale = HEAD_DIM ** -0.5
    hd = lambda a: a.reshape(b, -1, N_HEADS, HEAD_DIM)
    all_keys = lambda pst, new: hd(_with_past(pst, new, l_keys))
    near, far, causal = _near_tables(t5_table, tq)
    ones_cols = jnp.ones((b, l_keys, N_HEADS, N_EXTRA), F32)

    k_sb_all = jnp.transpose(hd(_with_past(pk_sb, k_sb, l_sb)), (0, 2, 1, 3)).astype(BF16)
    vt_sb_all = jnp.transpose(hd(_with_past(pv_sb, v_sb, l_sb)), (0, 2, 3, 1)).astype(BF16)
    later = (jnp.arange(SB_KB)[None, :] > jnp.arange(SB_KB)[:, None]).astype(BF16)
    o_sb = _sb_attn(jnp.transpose(hd(q_sb * scale), (0, 2, 3, 1)).astype(BF16), k_sb_all, vt_sb_all, later,
                    p_len, tq)

    half = jnp.arange(HEAD_DIM) < DIFF_DH
    q_df2 = jnp.where(jnp.stack([half, ~half])[None, None, None],
                      q_df.reshape(b, t, N_HEADS, 1, HEAD_DIM) * (DIFF_DH ** -0.5 * LOG2E), 0.0)
    lam_rows = jnp.zeros((SUBLANES, LANES), F32)
    for r, vec in enumerate((lam_q1, lam_k1, lam_q2, lam_k2)):
        lam_rows = lam_rows.at[r, :DIFF_DH].set(vec.astype(F32))
    lam_rows = lam_rows.at[4, :].set(lam_init)
    g_cols = jnp.broadcast_to(diff_subln.astype(F32)[:, None], (HEAD_DIM, tq))
    o_df = _df_attn(_aug_queries(q_df2.reshape(b, t, 2 * N_HEADS, HEAD_DIM),
                                 jnp.repeat(_split3(far[:N_HEADS]), 2, axis=0)),
                    _aug_keys(all_keys(pk_df, k_df), ones_cols, lp), _aug_values(all_keys(pv_df, v_df), lp),
                    near[:N_HEADS], lam_rows, g_cols, p_len, tq)

    kix_all = jnp.pad(_with_past(pk_ix, k_ix, l_keys), ((0, 0), (FRONT, lp - FRONT - l_keys), (0, 0))).astype(BF16)
    qix_t = jnp.transpose(q_ix.reshape(b, t, N_IDX_HEADS, IDX_DIM) * (IDX_DIM ** -0.5), (0, 2, 3, 1)).astype(BF16)
    upto = (jnp.arange(KB)[None, :] <= jnp.arange(KB)[:, None]).astype(BF16)
    o_ds = _ds_attn(qix_t, jnp.transpose(w_ix, (0, 2, 1)), kix_all,
                    _aug_queries(hd(q_ds * (scale * LOG2E)), _split3(far[N_HEADS:])),
                    _aug_keys(all_keys(pk_ds, k_ds), ones_cols, lp), _aug_values(all_keys(pv_ds, v_ds), lp),
                    near[N_HEADS:], upto, p_len, tq, l_keys, topk)

    c_all = _cumsum_seq(_with_past(plogf, logf, l_pad))
    c_keys = jnp.transpose(c_all[:, :, :l_keys], (0, 2, 1))
    o_fx = _fx_attn(_aug_queries(hd(q_fx * (scale * LOG2E)), jnp.ones((N_HEADS, N_EXTRA), F32)),
                    _aug_keys(all_keys(pk_fx, k_fx), _split3(-LOG2E * c_keys), lp),
                    _aug_values(all_keys(pv_fx, v_fx), lp), causal, p_len, tq)

    f = w_gate.shape[1]
    if pconv is None:
        pc = jnp.zeros((b, SUBLANES, f), F32)
    else:
        pc = jnp.pad(pconv.astype(F32), ((0, 0), (SUBLANES - (CONV_W - 1), 0), (0, 0)))
    cw = jnp.pad(conv_w.astype(F32), ((0, SUBLANES - CONV_W), (0, 0)))
    tr = lambda o: jnp.transpose(o, (0, 2, 1))
    x_new, st = _post(x, (tr(o_sb), tr(o_df), tr(o_ds), tr(o_fx)), w_out, g_ffn.reshape(1, d), w_gate, w_up,
                      w_down, cw, conv_b.reshape(1, f).astype(F32), pc, tm_post)

    rows = (hd(k_sb), hd(v_sb), k_df.reshape(b, t, N_HEADS, 2, DIFF_DH), hd(v_df), hd(k_ds), hd(v_ds),
            k_ix, hd(k_fx), hd(v_fx), logf, st[:, SUBLANES - (CONV_W - 1):])
    return x_new, rows


def _row_tile(n, cap):
    tm = min(n, cap)
    assert n % tm == 0
    return tm


def kernel(x_prompt, x_sample, cache_sb_k, cache_sb_v, cache_diff_k, cache_diff_v, cache_dsa_k, cache_dsa_v, cache_dsa_kidx, cache_fox_k, cache_fox_v, cache_fox_logf, state_ffn_conv, g_mix, w_in, b_forget, w_out, lam_q1, lam_k1, lam_q2, lam_k2, diff_subln, g_ffn, w_gate, w_up, w_down, conv_w, conv_b, t5_table, g_final):
    depth = w_in.shape[0]
    bp, tp, _ = x_prompt.shape
    bs, ts, _ = x_sample.shape
    xp, xs = x_prompt, x_sample
    p_rows, s_rows = [], []
    for l in range(depth):
        lw = (g_mix[l], _pack_w_in(w_in[l]), b_forget[l], w_out[l].astype(BF16), lam_q1[l], lam_k1[l],
              lam_q2[l], lam_k2[l], diff_subln[l], g_ffn[l], w_gate[l].astype(BF16), w_up[l].astype(BF16),
              w_down[l].astype(BF16), conv_w[l], conv_b[l])
        lam_init = 0.8 - 0.6 * math.exp(-0.3 * l)
        xp, rp = _layer(xp, None, lw, lam_init, t5_table, _row_tile(bp * tp, 512), _row_tile(tp, 256))
        s_past = (cache_sb_k[l], cache_sb_v[l], cache_diff_k[l], cache_diff_v[l], cache_dsa_k[l],
                  cache_dsa_v[l], cache_dsa_kidx[l], cache_fox_k[l], cache_fox_v[l], cache_fox_logf[l],
                  state_ffn_conv[l])
        xs, rs = _layer(xs, s_past, lw, lam_init, t5_table, _row_tile(bs * ts, 512), _row_tile(ts, 256))
        p_rows.append(rp)
        s_rows.append(rs)
    y_prompt = _final_norm(xp, g_final, _row_tile(bp * tp, 512))
    y_sample = _final_norm(xs, g_final, _row_tile(bs * ts, 512))
    stack = lambda rows, i: jnp.stack([r[i] for r in rows], axis=0)
    return ((y_prompt, y_sample) + tuple(stack(p_rows, i) for i in range(11))
            + tuple(stack(s_rows, i) for i in range(11)))
```

```python
import functools
import math

import jax
import jax.numpy as jnp
from jax import lax
from jax.experimental import pallas as pl
from jax.experimental.pallas import tpu as pltpu

F32 = jnp.float32
BF16 = jnp.bfloat16
I32 = jnp.int32
I16 = jnp.int16

N_HEADS = 4
HEAD_DIM = 64
GROUP_WIDTH = N_HEADS * HEAD_DIM
DIFF_DH = HEAD_DIM // 2
N_IDX_HEADS = 8
IDX_DIM = 64
CHUNK = 64
TOPK_MAX = 256
T5_BUCKETS = 32
T5_MAX_DIST = 128
CONV_W = 3
RMS_EPS = 1e-6
SUBLN_EPS = 1e-5

LANES = 128
SUBLANES = 8
SB_KB = 256
KB = 512
FRONT = 256
TQ_MAX = 256
N_EXTRA = 3
COL_FLAG = HEAD_DIM + N_EXTRA
V_ROWS = 80
NEG = -1e30
SB_DEAD = -110.0
HALF_BITS = 16
HALF_BIAS = 2 ** (HALF_BITS - 1)
PACKED_ROWS = 16
KEY_NEG_INF = -2139095041
LOG2E = math.log2(math.e)
VMEM_LIMIT = 56 * 1024 * 1024

COL_QIX = 9 * GROUP_WIDTH
COL_SLAB = COL_QIX + N_IDX_HEADS * IDX_DIM
SLAB_WIX = IDX_DIM
SLAB_GATE = IDX_DIM + N_IDX_HEADS
COL_FX = COL_SLAB + LANES
PROJ_PAD = COL_FX + 3 * GROUP_WIDTH


def _cparams(sem):
    return pltpu.CompilerParams(dimension_semantics=sem, vmem_limit_bytes=VMEM_LIMIT)


def _resident(shape, index_map):
    return pl.BlockSpec(shape, index_map, pipeline_mode=pl.Buffered(1))


def _full(shape):
    return _resident(shape, lambda b, i: (0,) * len(shape))


def _log_sigmoid(z):
    return jnp.minimum(z, 0.0) - jnp.log(1.0 + jnp.exp(-jnp.abs(z)))


def _dot(a, b):
    return jnp.dot(a, b, preferred_element_type=F32)


def _proj_kernel(x_ref, g_ref, w_ref, bf_ref, y_ref, lf_ref):
    x = x_ref[...]
    hn = x * lax.rsqrt(jnp.mean(x * x, axis=-1, keepdims=True) + RMS_EPS) * g_ref[...]
    y = _dot(hn.astype(BF16), w_ref[...])
    y_ref[...] = y
    lf_ref[...] = _log_sigmoid(y[:, COL_SLAB:COL_SLAB + LANES] + bf_ref[...])


def _proj(x2, g, w, bf, tm):
    n, d = x2.shape
    return pl.pallas_call(
        _proj_kernel,
        grid=(n // tm,),
        in_specs=[pl.BlockSpec((tm, d), lambda i: (i, 0)),
                  pl.BlockSpec((1, d), lambda i: (0, 0)),
                  _resident((d, PROJ_PAD), lambda i: (0, 0)),
                  pl.BlockSpec((1, LANES), lambda i: (0, 0))],
        out_specs=[pl.BlockSpec((tm, PROJ_PAD), lambda i: (i, 0)),
                   pl.BlockSpec((tm, LANES), lambda i: (i, 0))],
        out_shape=[jax.ShapeDtypeStruct((n, PROJ_PAD), F32),
                   jax.ShapeDtypeStruct((n, LANES), F32)],
        compiler_params=_cparams(("parallel",)),
        name="proj",
    )(x2, g, w, bf)


def _cumsum_kernel(x_ref, o_ref):
    nch, rows, _ = x_ref.shape
    lane = lax.broadcasted_iota(I32, (rows, LANES), 1)

    def body(c, carry):
        x = x_ref[c]
        d = 1
        while d < LANES:
            x = x + jnp.where(lane >= d, pltpu.roll(x, d, 1), 0.0)
            d *= 2
        x = x + carry
        o_ref[c] = x
        return jnp.broadcast_to(x[:, LANES - 1:LANES], (rows, LANES))

    lax.fori_loop(0, nch, body, jnp.zeros((rows, LANES), F32))


def _cumsum_seq(logf_all):
    b, l, h = logf_all.shape
    rows = b * h
    rows_pad = -(-rows // SUBLANES) * SUBLANES
    x = jnp.transpose(logf_all, (0, 2, 1)).reshape(rows, l // LANES, LANES)
    x = jnp.pad(jnp.transpose(x, (1, 0, 2)), ((0, 0), (0, rows_pad - rows), (0, 0)))
    c = pl.pallas_call(
        _cumsum_kernel,
        out_shape=jax.ShapeDtypeStruct(x.shape, F32),
        compiler_params=pltpu.CompilerParams(vmem_limit_bytes=VMEM_LIMIT),
        name="cumsum",
    )(x)
    return jnp.transpose(c[:, :rows], (1, 0, 2)).reshape(b, h, l)


def _sb_kernel(qt_ref, k_ref, vt_ref, later_ref, o_ref, run_scr, acc_scr, *, past, tq):
    q0 = pl.multiple_of(past + pl.program_id(1) * tq, SB_KB)
    run_scr[...] = jnp.zeros(run_scr.shape, F32)
    acc_scr[...] = jnp.zeros(acc_scr.shape, F32)
    later = later_ref[...]

    def block(start, strict):
        logits = [_dot(k_ref[0, h, pl.ds(start, SB_KB), :], qt_ref[0, h]) for h in range(N_HEADS)]
        for h in range(N_HEADS):
            z = logits[h]
            ls = _log_sigmoid(z)
            lk = ls - z
            if strict is not None:
                lk = jnp.where(strict, lk, 0.0)
            hi = lk.astype(BF16)
            rem = lk - hi.astype(F32)
            mid = rem.astype(BF16)
            lo = (rem - mid.astype(F32)).astype(BF16)
            between = _dot(later, hi) + _dot(later, mid) + _dot(later, lo) + run_scr[h]
            a = jnp.exp(ls + between)
            if strict is not None:
                a = jnp.where(strict, a, 0.0)
            acc_scr[h] = acc_scr[h] + _dot(vt_ref[0, h, :, pl.ds(start, SB_KB)], a.astype(BF16))
            run_scr[h] = run_scr[h] + jnp.sum(lk, axis=0, keepdims=True)

    block(q0, lax.broadcasted_iota(I32, (SB_KB, tq), 0) < lax.broadcasted_iota(I32, (SB_KB, tq), 1))

    def cond(carry):
        return (carry[0] >= 0) & (carry[1] > SB_DEAD)

    def body(carry):
        block(pl.multiple_of(carry[0], SB_KB), None)
        return carry[0] - SB_KB, jnp.max(run_scr[...])

    lax.while_loop(cond, body, (q0 - SB_KB, jnp.max(run_scr[...])))
    for h in range(N_HEADS):
        o_ref[0, h * HEAD_DIM:(h + 1) * HEAD_DIM, :] = acc_scr[h]


def _sb_attn(qt, k, vt, later, past, tq):
    b, _, _, t = qt.shape
    l_sb = k.shape[2]
    return pl.pallas_call(
        functools.partial(_sb_kernel, past=past, tq=tq),
        grid=(b, t // tq),
        in_specs=[pl.BlockSpec((1, N_HEADS, HEAD_DIM, tq), lambda b, i: (b, 0, 0, i)),
                  _resident((1, N_HEADS, l_sb, HEAD_DIM), lambda b, i: (b, 0, 0, 0)),
                  _resident((1, N_HEADS, HEAD_DIM, l_sb), lambda b, i: (b, 0, 0, 0)),
                  _full(later.shape)],
        out_specs=pl.BlockSpec((1, GROUP_WIDTH, tq), lambda b, i: (b, 0, i)),
        out_shape=jax.ShapeDtypeStruct((b, GROUP_WIDTH, t), F32),
        scratch_shapes=[pltpu.VMEM((N_HEADS, 1, tq), F32), pltpu.VMEM((N_HEADS, HEAD_DIM, tq), F32)],
        compiler_params=_cparams(("parallel", "parallel")),
        name="stick_breaking",
    )(qt, k, vt, later)


def _n_far_blocks(q0):
    return jnp.maximum(q0 - FRONT + KB - 1, 0) // KB


def _far_start(q0, j):
    return pl.multiple_of(q0 - KB * (j + 1), FRONT)


def _flash_tile(n_maps, head_of, qa_ref, ka_ref, va_ref, state, q0, near_add, far_add):
    m_scr, acc_scr, s_scr, smax_scr = state
    m_scr[...] = jnp.full(m_scr.shape, NEG, F32)
    acc_scr[...] = jnp.zeros(acc_scr.shape, F32)

    def start_of(i):
        return pl.multiple_of(q0 - KB * i, FRONT)

    def scores(mp, start, add):
        s = _dot(ka_ref[0, head_of(mp), pl.ds(start, KB), :], qa_ref[0, mp])
        if add is not None:
            s = s + add(mp)
        s_scr[mp] = s
        smax_scr[mp] = jnp.max(s, axis=0, keepdims=True)

    def absorb(mp, start):
        m_old = m_scr[mp]
        m_new = jnp.maximum(m_old, smax_scr[mp])
        p = jnp.exp2(s_scr[mp] - m_new).astype(BF16)
        acc_scr[mp] = (acc_scr[mp] * jnp.exp2(m_old - m_new)
                       + _dot(va_ref[0, head_of(mp), :, pl.ds(start, KB)], p))
        m_scr[mp] = m_new

    for mp in range(n_maps):
        scores(mp, start_of(0), near_add)

    def body(i, c):
        nxt = start_of(i + 1)
        for mp in range(n_maps):
            absorb(mp, start_of(i))
            scores(mp, nxt, None if far_add is None else (lambda mp_: far_add(nxt)))
        return c

    n_far = _n_far_blocks(q0)
    lax.fori_loop(0, n_far, body, 0)
    for mp in range(n_maps):
        absorb(mp, start_of(n_far))
    return acc_scr


def _normalized(acc_scr, mp):
    return acc_scr[mp, 0:HEAD_DIM, :] / acc_scr[mp, HEAD_DIM:HEAD_DIM + 1, :]


def _flash_specs(n_maps, tq, lp):
    qa_spec = pl.BlockSpec((1, n_maps, LANES, tq), lambda b, i: (b, 0, 0, i))
    ka_spec = _resident((1, N_HEADS, lp, LANES), lambda b, i: (b, 0, 0, 0))
    va_spec = _resident((1, N_HEADS, V_ROWS, lp), lambda b, i: (b, 0, 0, 0))
    o_spec = pl.BlockSpec((1, GROUP_WIDTH, tq), lambda b, i: (b, 0, i))
    state = [pltpu.VMEM((n_maps, 1, tq), F32), pltpu.VMEM((n_maps, V_ROWS, tq), F32),
             pltpu.VMEM((n_maps, KB, tq), F32), pltpu.VMEM((n_maps, 1, tq), F32)]
    return qa_spec, ka_spec, va_spec, o_spec, state


def _fx_kernel(qa_ref, ka_ref, va_ref, causal_ref, o_ref, *state, past, tq):
    q0 = past + pl.program_id(1) * tq
    acc_scr = _flash_tile(N_HEADS, lambda mp: mp, qa_ref, ka_ref, va_ref, state, q0,
                          lambda mp: causal_ref[...], None)
    for h in range(N_HEADS):
        o_ref[0, h * HEAD_DIM:(h + 1) * HEAD_DIM, :] = _normalized(acc_scr, h)


def _fx_attn(qa, ka, va, causal, past, tq):
    b, _, _, t = qa.shape
    qa_spec, ka_spec, va_spec, o_spec, scratch = _flash_specs(N_HEADS, tq, ka.shape[2])
    return pl.pallas_call(
        functools.partial(_fx_kernel, past=past, tq=tq),
        grid=(b, t // tq),
        in_specs=[qa_spec, ka_spec, va_spec, _full(causal.shape)],
        out_specs=o_spec,
        out_shape=jax.ShapeDtypeStruct((b, GROUP_WIDTH, t), F32),
        scratch_shapes=scratch,
        compiler_params=_cparams(("parallel", "parallel")),
        name="forgetting",
    )(qa, ka, va, causal)


def _df_kernel(qa_ref, ka_ref, va_ref, near_ref, lam_ref, g_ref, o_ref, *state, past, tq):
    q0 = past + pl.program_id(1) * tq
    acc_scr = _flash_tile(2 * N_HEADS, lambda mp: mp // 2, qa_ref, ka_ref, va_ref, state, q0,
                          lambda mp: near_ref[mp // 2], None)
    lam_init = lam_ref[4:5, 0:1]
    lam = (jnp.exp(jnp.sum(lam_ref[0:1, :] * lam_ref[1:2, :], axis=-1, keepdims=True))
           - jnp.exp(jnp.sum(lam_ref[2:3, :] * lam_ref[3:4, :], axis=-1, keepdims=True)) + lam_init)
    for h in range(N_HEADS):
        o = _normalized(acc_scr, 2 * h) - lam * _normalized(acc_scr, 2 * h + 1)
        o = o * lax.rsqrt(jnp.mean(o * o, axis=0, keepdims=True) + SUBLN_EPS) * g_ref[...]
        o_ref[0, h * HEAD_DIM:(h + 1) * HEAD_DIM, :] = o * (1.0 - lam_init)


def _df_attn(qa, ka, va, near, lam_rows, g_cols, past, tq):
    b, _, _, t = qa.shape
    qa_spec, ka_spec, va_spec, o_spec, scratch = _flash_specs(2 * N_HEADS, tq, ka.shape[2])
    return pl.pallas_call(
        functools.partial(_df_kernel, past=past, tq=tq),
        grid=(b, t // tq),
        in_specs=[qa_spec, ka_spec, va_spec, _full(near.shape), _full(lam_rows.shape), _full(g_cols.shape)],
        out_specs=o_spec,
        out_shape=jax.ShapeDtypeStruct((b, GROUP_WIDTH, t), F32),
        scratch_shapes=scratch,
        compiler_params=_cparams(("parallel", "parallel")),
        name="differential",
    )(qa, ka, va, near, lam_rows, g_cols)


def _ds_kernel(qix_ref, wix_ref, kix_ref, qa_ref, ka_ref, va_ref, near_ref, upto_ref, o_ref,
               key_scr, k16_scr, msk_scr, *state, past, tq, l_keys, topk):
    q0 = past + pl.program_id(1) * tq
    near0 = pl.multiple_of(q0, FRONT)
    n_far = _n_far_blocks(q0)
    wix = wix_ref[0] * (N_IDX_HEADS ** -0.5)

    def score_block(start, near):
        kix = kix_ref[0, pl.ds(start, KB), :]
        sc = jnp.zeros((KB, tq), F32)
        for hh in range(N_IDX_HEADS):
            sc = sc + wix[hh:hh + 1, :] * jnp.maximum(_dot(kix, qix_ref[0, hh]), 0.0)
        kidx = start + lax.broadcasted_iota(I32, (KB, tq), 0)
        ok = (kidx >= FRONT) & (kidx < FRONT + l_keys)
        if near:
            ok = ok & (near_ref[0] > 0.5 * NEG)
        bits = lax.bitcast_convert_type(jnp.where(ok, sc, -jnp.inf), I32)
        key = bits ^ ((bits >> 31) & 0x7FFFFFFF)
        key_scr[pl.ds(start, KB), :] = key
        k16_scr[pl.ds(start, KB), :] = (key >> HALF_BITS).astype(I16)

    def each_block(fn):
        fn(near0)

        def far(j, c):
            fn(_far_start(q0, j))
            return c

        lax.fori_loop(0, n_far, far, 0)

    score_block(near0, True)

    def score_far(j, c):
        score_block(_far_start(q0, j), False)
        return c

    lax.fori_loop(0, n_far, score_far, 0)

    def count16(cand):
        cand16 = cand.astype(I16)

        def blk(start):
            ones = jnp.where(k16_scr[pl.ds(start, KB), :] >= cand16, jnp.ones((), BF16), jnp.zeros((), BF16))
            parts = [ones[r:r + PACKED_ROWS] for r in range(0, KB, PACKED_ROWS)]
            while len(parts) > 1:
                parts = [parts[r] + parts[r + 1] for r in range(0, len(parts), 2)]
            return parts[0].astype(F32)

        acc = lax.fori_loop(0, n_far, lambda j, c: c + blk(_far_start(q0, j)), blk(near0))
        return jnp.sum(acc, axis=0, keepdims=True)

    def greedy16():
        def bit_body(i, u):
            cand = u | lax.shift_left(jnp.int32(1), HALF_BITS - 1 - i)
            return jnp.where(count16(cand - HALF_BIAS) >= topk, cand, u)
        return lax.fori_loop(0, HALF_BITS, bit_body, jnp.zeros((1, tq), I32))

    thr_hi = greedy16() - HALF_BIAS

    def refine_block(start):
        key = key_scr[pl.ds(start, KB), :]
        hi = key >> HALF_BITS
        low = jnp.where(hi > thr_hi, HALF_BIAS - 1,
                        jnp.where(hi < thr_hi, -HALF_BIAS, (key & (2 * HALF_BIAS - 1)) - HALF_BIAS))
        k16_scr[pl.ds(start, KB), :] = low.astype(I16)

    each_block(refine_block)
    thr = thr_hi * (2 * HALF_BIAS) + greedy16()

    def count2(start):
        key = key_scr[pl.ds(start, KB), :]
        return (jnp.sum((key >= thr).astype(I32), axis=0, keepdims=True),
                jnp.sum((key > thr).astype(I32), axis=0, keepdims=True))

    def count2_far(j, c):
        ge, gt = count2(_far_start(q0, j))
        return c[0] + ge, c[1] + gt

    n_ge, n_gt = lax.fori_loop(0, n_far, count2_far, count2(near0))
    tied = jnp.max(n_ge) > topk

    @pl.when(jnp.logical_not(tied))
    def _():
        def plain_block(start):
            key = key_scr[pl.ds(start, KB), :]
            msk_scr[pl.ds(start, KB), :] = jnp.where((key >= thr) & (key > KEY_NEG_INF), 0.0, NEG)
        each_block(plain_block)

    @pl.when(tied)
    def _():
        need = (topk - n_gt).astype(F32)

        def mask_block(start, seen):
            key = key_scr[pl.ds(start, KB), :]
            tie = key == thr
            rank = _dot(upto_ref[...], tie.astype(BF16)) + seen
            sel = (key > KEY_NEG_INF) & ((key > thr) | (tie & (rank <= need)))
            msk_scr[pl.ds(start, KB), :] = jnp.where(sel, 0.0, NEG)
            return seen + jnp.sum(tie.astype(F32), axis=0, keepdims=True)

        seen = lax.fori_loop(0, n_far, lambda i, c: mask_block(_far_start(q0, n_far - 1 - i), c),
                             jnp.zeros((1, tq), F32))
        mask_block(near0, seen)

    acc_scr = _flash_tile(N_HEADS, lambda mp: mp, qa_ref, ka_ref, va_ref, state, q0,
                          lambda mp: near_ref[mp] + msk_scr[pl.ds(near0, KB), :],
                          lambda start: msk_scr[pl.ds(start, KB), :])
    for h in range(N_HEADS):
        o_ref[0, h * HEAD_DIM:(h + 1) * HEAD_DIM, :] = _normalized(acc_scr, h)


def _ds_attn(qix, wix, kix, qa, ka, va, near, upto, past, tq, l_keys, topk):
    b, _, _, t = qa.shape
    lp = ka.shape[2]
    qa_spec, ka_spec, va_spec, o_spec, scratch = _flash_specs(N_HEADS, tq, lp)
    return pl.pallas_call(
        functools.partial(_ds_kernel, past=past, tq=tq, l_keys=l_keys, topk=topk),
        grid=(b, t // tq),
        in_specs=[pl.BlockSpec((1, N_IDX_HEADS, IDX_DIM, tq), lambda b, i: (b, 0, 0, i)),
                  pl.BlockSpec((1, N_IDX_HEADS, tq), lambda b, i: (b, 0, i)),
                  _resident((1, lp, IDX_DIM), lambda b, i: (b, 0, 0)),
                  qa_spec, ka_spec, va_spec, _full(near.shape), _full(upto.shape)],
        out_specs=o_spec,
        out_shape=jax.ShapeDtypeStruct((b, GROUP_WIDTH, t), F32),
        scratch_shapes=[pltpu.VMEM((lp, tq), I32), pltpu.VMEM((lp, tq), I16), pltpu.VMEM((lp, tq), F32)] + scratch,
        compiler_params=_cparams(("parallel", "parallel")),
        name="indexed_sparse",
    )(qix, wix, kix, qa, ka, va, near, upto)


def _post_kernel(x_ref, o0_ref, o1_ref, o2_ref, o3_ref, wo_ref, g_ref, wg_ref, wu_ref, wd_ref,
                 cw_ref, cb_ref, pc_ref, xo_ref, st_ref, uext, *, tm):
    t = pl.program_id(1)

    @pl.when(t == 0)
    def _():
        uext[0:SUBLANES, :] = pc_ref[0]

    @pl.when(t > 0)
    def _():
        uext[0:SUBLANES, :] = uext[tm:tm + SUBLANES, :]

    x = x_ref[0]
    for gi, o_ref in enumerate((o0_ref, o1_ref, o2_ref, o3_ref)):
        x = x + _dot(o_ref[0].astype(BF16), wo_ref[gi * GROUP_WIDTH:(gi + 1) * GROUP_WIDTH, :])
    hn = (x * lax.rsqrt(jnp.mean(x * x, axis=-1, keepdims=True) + RMS_EPS) * g_ref[...]).astype(BF16)
    u = _dot(hn, wg_ref[...])
    uext[SUBLANES:SUBLANES + tm, :] = u
    conv = cb_ref[...]
    conv = conv + cw_ref[0:1, :] * uext[SUBLANES - 2:SUBLANES - 2 + tm, :]
    conv = conv + cw_ref[1:2, :] * uext[SUBLANES - 1:SUBLANES - 1 + tm, :]
    conv = conv + cw_ref[2:3, :] * u
    hidden = conv * (1.0 / (1.0 + jnp.exp(-conv))) * _dot(hn, wu_ref[...])
    xo_ref[0] = x + _dot(hidden.astype(BF16), wd_ref[...])
    st_ref[0] = uext[tm:tm + SUBLANES, :]


def _post(x, outs, wo, g, wg, wu, wd, cw, cb, pconv, tm):
    b, t, d = x.shape
    f = wg.shape[1]
    row = lambda w: pl.BlockSpec((1, tm, w), lambda bi, ti: (bi, ti, 0))
    const = lambda shape: _resident(shape, lambda bi, ti: (0,) * len(shape))
    return pl.pallas_call(
        functools.partial(_post_kernel, tm=tm),
        grid=(b, t // tm),
        in_specs=[row(d)] + [row(GROUP_WIDTH)] * 4 + [
            const(wo.shape), const(g.shape), const(wg.shape), const(wu.shape), const(wd.shape),
            const(cw.shape), const(cb.shape),
            pl.BlockSpec((1, SUBLANES, f), lambda bi, ti: (bi, 0, 0))],
        out_specs=[row(d), pl.BlockSpec((1, SUBLANES, f), lambda bi, ti: (bi, 0, 0))],
        out_shape=[jax.ShapeDtypeStruct((b, t, d), F32),
                   jax.ShapeDtypeStruct((b, SUBLANES, f), F32)],
        scratch_shapes=[pltpu.VMEM((tm + SUBLANES, f), F32)],
        compiler_params=_cparams(("parallel", "arbitrary")),
        name="out_proj_mlp",
    )(x, *outs, wo, g, wg, wu, wd, cw, cb, pconv)


def _final_norm_kernel(x_ref, g_ref, o_ref):
    x = x_ref[...]
    o_ref[...] = x * lax.rsqrt(jnp.mean(x * x, axis=-1, keepdims=True) + RMS_EPS) * g_ref[...]


def _final_norm(x, g, tm):
    b, t, d = x.shape
    x2 = x.reshape(b * t, d)
    y = pl.pallas_call(
        _final_norm_kernel,
        grid=(b * t // tm,),
        in_specs=[pl.BlockSpec((tm, d), lambda i: (i, 0)), pl.BlockSpec((1, d), lambda i: (0, 0))],
        out_specs=pl.BlockSpec((tm, d), lambda i: (i, 0)),
        out_shape=jax.ShapeDtypeStruct(x2.shape, F32),
        compiler_params=_cparams(("parallel",)),
        name="final_norm",
    )(x2, g.reshape(1, d))
    return y.reshape(b, t, d)


def _t5_bucket(rel):
    half = T5_BUCKETS // 2
    max_exact = half // 2
    base = jnp.where(rel > 0, half, 0)
    n = jnp.abs(rel)
    nf = jnp.maximum(n, 1).astype(F32)
    large = max_exact + (jnp.log(nf / max_exact) / math.log(T5_MAX_DIST / max_exact)
                         * (half - max_exact)).astype(I32)
    large = jnp.minimum(large, half - 1)
    return base + jnp.where(n < max_exact, n, large)


def _bf16_part(x):
    bits = lax.bitcast_convert_type(x, jnp.uint32) & jnp.uint32(0xFFFF0000)
    return lax.bitcast_convert_type(bits, F32)


def _split3(x):
    hi = _bf16_part(x)
    mid = _bf16_part(x - hi)
    lo = _bf16_part(x - hi - mid)
    return jnp.stack([hi, mid, lo], axis=-1)


def _near_tables(t5_table, tq):
    kk = jnp.arange(KB, dtype=I32)[:, None] - FRONT
    qq = jnp.arange(tq, dtype=I32)[None, :]
    tab = t5_table.astype(F32).T
    far = tab[:, _t5_bucket(jnp.int32(-T5_MAX_DIST))]
    one_hot = (_t5_bucket(kk - qq)[:, :, None] == jnp.arange(T5_BUCKETS, dtype=I32)).astype(F32)
    bias = jnp.einsum("kqb,hb->hkq", one_hot, tab, precision=lax.Precision.HIGHEST)
    chunk_ok = (kk // CHUNK) <= (qq // CHUNK)
    near = jnp.where(chunk_ok[None], (bias - far[:, None, None]) * LOG2E, NEG)
    causal = jnp.where(kk <= qq, 0.0, NEG).astype(F32)
    return near, far * LOG2E, causal


def _pack_w_in(w_in):
    d = w_in.shape[0]
    fx0 = COL_SLAB + IDX_DIM + N_IDX_HEADS
    pad = jnp.zeros((d, LANES - SLAB_GATE - N_HEADS), w_in.dtype)
    return jnp.concatenate([w_in[:, :fx0], w_in[:, fx0 + 3 * GROUP_WIDTH:], pad,
                            w_in[:, fx0:fx0 + 3 * GROUP_WIDTH]], axis=1).astype(BF16)


def _with_past(past, new, l_pad):
    b, t = new.shape[:2]
    new = new.reshape(b, t, -1)
    if past is not None:
        new = jnp.concatenate([past.reshape(b, past.shape[1], -1).astype(F32), new], axis=1)
    return jnp.pad(new, ((0, 0), (0, l_pad - new.shape[1]), (0, 0)))


def _aug_queries(q, rows):
    b, t, m, _ = q.shape
    extra = jnp.zeros((m, LANES - HEAD_DIM), F32).at[:, :N_EXTRA].set(rows).at[:, N_EXTRA].set(1.0)
    extra = jnp.broadcast_to(extra[None, :, :, None], (b, m, LANES - HEAD_DIM, t))
    return jnp.concatenate([jnp.transpose(q, (0, 2, 3, 1)), extra], axis=2).astype(BF16)


def _aug_keys(k, cols, lp):
    b, l, h, _ = k.shape
    body = jnp.concatenate([k, cols, jnp.zeros((b, l, h, LANES - COL_FLAG), F32)], axis=-1)
    pad_row = jnp.zeros((LANES,), F32).at[COL_FLAG].set(NEG)
    pad = lambda n: jnp.broadcast_to(pad_row, (b, n, h, LANES))
    full = jnp.concatenate([pad(FRONT), body, pad(lp - FRONT - l)], axis=1)
    return jnp.transpose(full, (0, 2, 1, 3)).astype(BF16)


def _aug_values(v, lp):
    b, l, h, _ = v.shape
    vt = jnp.transpose(v, (0, 2, 3, 1))
    body = jnp.concatenate([vt, jnp.ones((b, h, 1, l), F32),
                            jnp.zeros((b, h, V_ROWS - HEAD_DIM - 1, l), F32)], axis=2)
    return jnp.pad(body, ((0, 0), (0, 0), (0, 0), (FRONT, lp - FRONT - l))).astype(BF16)


def _layer(x, past, lw, lam_init, t5_table, tm_proj, tm_post):
    (g_mix, w_in_p, b_forget, w_out, lam_q1, lam_k1, lam_q2, lam_k2, diff_subln,
     g_ffn, w_gate, w_up, w_down, conv_w, conv_b) = lw
    b, t, d = x.shape
    p_len = 0 if past is None else past[0].shape[1]
    l_keys = p_len + t
    l_pad = -(-l_keys // LANES) * LANES
    topk = min(TOPK_MAX, l_keys // 4)
    tq = min(t, TQ_MAX)
    assert p_len % FRONT == 0 and (t % TQ_MAX == 0 or t <= LANES) and FRONT == SB_KB == TQ_MAX
    lp = -(-max(FRONT + l_keys, p_len + t - tq + KB) // LANES) * LANES
    l_sb = p_len + t - tq + SB_KB
    (pk_sb, pv_sb, pk_df, pv_df, pk_ds, pv_ds, pk_ix, pk_fx, pv_fx, plogf, pconv) = (
        past if past is not None else (None,) * 11)

    bf = jnp.zeros((1, LANES), F32).at[0, SLAB_GATE:SLAB_GATE + N_HEADS].set(b_forget)
    y, lf = _proj(x.reshape(b * t, d), g_mix.reshape(1, d), w_in_p, bf, tm_proj)
    y = y.reshape(b, t, PROJ_PAD)
    grp = lambda i: y[:, :, i * GROUP_WIDTH:(i + 1) * GROUP_WIDTH]
    q_sb, k_sb, v_sb, q_df, k_df, v_df, q_ds, k_ds, v_ds = [grp(i) for i in range(9)]
    q_ix = y[:, :, COL_QIX:COL_SLAB]
    slab = y[:, :, COL_SLAB:COL_FX]
    k_ix = slab[:, :, :IDX_DIM]
    w_ix = slab[:, :, SLAB_WIX:SLAB_WIX + N_IDX_HEADS]
    q_fx, k_fx, v_fx = [y[:, :, COL_FX + i * GROUP_WIDTH:COL_FX + (i + 1) * GROUP_WIDTH] for i in range(3)]
    logf = lf.reshape(b, t, LANES)[:, :, SLAB_GATE:SLAB_GATE + N_HEADS]

    scale = HEAD_DIM ** -0.5
    hd = lambda a: a.reshape(b, -1, N_HEADS, HEAD_DIM)
    all_keys = lambda pst, new: hd(_with_past(pst, new, l_keys))
    near, far, causal = _near_tables(t5_table, tq)
    ones_cols = jnp.ones((b, l_keys, N_HEADS, N_EXTRA), F32)

    k_sb_all = jnp.transpose(hd(_with_past(pk_sb, k_sb, l_sb)), (0, 2, 1, 3)).astype(BF16)
    vt_sb_all = jnp.transpose(hd(_with_past(pv_sb, v_sb, l_sb)), (0, 2, 3, 1)).astype(BF16)
    later = (jnp.arange(SB_KB)[None, :] > jnp.arange(SB_KB)[:, None]).astype(BF16)
    o_sb = _sb_attn(jnp.transpose(hd(q_sb * scale), (0, 2, 3, 1)).astype(BF16), k_sb_all, vt_sb_all, later,
                    p_len, tq)

    half = jnp.arange(HEAD_DIM) < DIFF_DH
    q_df2 = jnp.where(jnp.stack([half, ~half])[None, None, None],
                      q_df.reshape(b, t, N_HEADS, 1, HEAD_DIM) * (DIFF_DH ** -0.5 * LOG2E), 0.0)
    lam_rows = jnp.zeros((SUBLANES, LANES), F32)
    for r, vec in enumerate((lam_q1, lam_k1, lam_q2, lam_k2)):
        lam_rows = lam_rows.at[r, :DIFF_DH].set(vec.astype(F32))
    lam_rows = lam_rows.at[4, :].set(lam_init)
    g_cols = jnp.broadcast_to(diff_subln.astype(F32)[:, None], (HEAD_DIM, tq))
    o_df = _df_attn(_aug_queries(q_df2.reshape(b, t, 2 * N_HEADS, HEAD_DIM),
                                 jnp.repeat(_split3(far[:N_HEADS]), 2, axis=0)),
                    _aug_keys(all_keys(pk_df, k_df), ones_cols, lp), _aug_values(all_keys(pv_df, v_df), lp),
                    near[:N_HEADS], lam_rows, g_cols, p_len, tq)

    kix_all = jnp.pad(_with_past(pk_ix, k_ix, l_keys), ((0, 0), (FRONT, lp - FRONT - l_keys), (0, 0))).astype(BF16)
    qix_t = jnp.transpose(q_ix.reshape(b, t, N_IDX_HEADS, IDX_DIM) * (IDX_DIM ** -0.5), (0, 2, 3, 1)).astype(BF16)
    upto = (jnp.arange(KB)[None, :] <= jnp.arange(KB)[:, None]).astype(BF16)
    o_ds = _ds_attn(qix_t, jnp.transpose(w_ix, (0, 2, 1)), kix_all,
                    _aug_queries(hd(q_ds * (scale * LOG2E)), _split3(far[N_HEADS:])),
                    _aug_keys(all_keys(pk_ds, k_ds), ones_cols, lp), _aug_values(all_keys(pv_ds, v_ds), lp),
                    near[N_HEADS:], upto, p_len, tq, l_keys, topk)

    c_all = _cumsum_seq(_with_past(plogf, logf, l_pad))
    c_keys = jnp.transpose(c_all[:, :, :l_keys], (0, 2, 1))
    o_fx = _fx_attn(_aug_queries(hd(q_fx * (scale * LOG2E)), jnp.ones((N_HEADS, N_EXTRA), F32)),
                    _aug_keys(all_keys(pk_fx, k_fx), _split3(-LOG2E * c_keys), lp),
                    _aug_values(all_keys(pv_fx, v_fx), lp), causal, p_len, tq)

    f = w_gate.shape[1]
    if pconv is None:
        pc = jnp.zeros((b, SUBLANES, f), F32)
    else:
        pc = jnp.pad(pconv.astype(F32), ((0, 0), (SUBLANES - (CONV_W - 1), 0), (0, 0)))
    cw = jnp.pad(conv_w.astype(F32), ((0, SUBLANES - CONV_W), (0, 0)))
    tr = lambda o: jnp.transpose(o, (0, 2, 1))
    x_new, st = _post(x, (tr(o_sb), tr(o_df), tr(o_ds), tr(o_fx)), w_out, g_ffn.reshape(1, d), w_gate, w_up,
                      w_down, cw, conv_b.reshape(1, f).astype(F32), pc, tm_post)

    rows = (hd(k_sb), hd(v_sb), k_df.reshape(b, t, N_HEADS, 2, DIFF_DH), hd(v_df), hd(k_ds), hd(v_ds),
            k_ix, hd(k_fx), hd(v_fx), logf, st[:, SUBLANES - (CONV_W - 1):])
    return x_new, rows


def _row_tile(n, cap):
    tm = min(n, cap)
    assert n % tm == 0
    return tm


def kernel(x_prompt, x_sample, cache_sb_k, cache_sb_v, cache_diff_k, cache_diff_v, cache_dsa_k, cache_dsa_v, cache_dsa_kidx, cache_fox_k, cache_fox_v, cache_fox_logf, state_ffn_conv, g_mix, w_in, b_forget, w_out, lam_q1, lam_k1, lam_q2, lam_k2, diff_subln, g_ffn, w_gate, w_up, w_down, conv_w, conv_b, t5_table, g_final):
    depth = w_in.shape[0]
    bp, tp, _ = x_prompt.shape
    bs, ts, _ = x_sample.shape
    xp, xs = x_prompt, x_sample
    p_rows, s_rows = [], []
    for l in range(depth):
        lw = (g_mix[l], _pack_w_in(w_in[l]), b_forget[l], w_out[l].astype(BF16), lam_q1[l], lam_k1[l],
              lam_q2[l], lam_k2[l], diff_subln[l], g_ffn[l], w_gate[l].astype(BF16), w_up[l].astype(BF16),
              w_down[l].astype(BF16), conv_w[l], conv_b[l])
        lam_init = 0.8 - 0.6 * math.exp(-0.3 * l)
        xp, rp = _layer(xp, None, lw, lam_init, t5_table, _row_tile(bp * tp, 512), _row_tile(tp, 256))
        s_past = (cache_sb_k[l], cache_sb_v[l], cache_diff_k[l], cache_diff_v[l], cache_dsa_k[l],
                  cache_dsa_v[l], cache_dsa_kidx[l], cache_fox_k[l], cache_fox_v[l], cache_fox_logf[l],
                  state_ffn_conv[l])
        xs, rs = _layer(xs, s_past, lw, lam_init, t5_table, _row_tile(bs * ts, 512), _row_tile(ts, 256))
        p_rows.append(rp)
        s_rows.append(rs)
    y_prompt = _final_norm(xp, g_final, _row_tile(bp * tp, 512))
    y_sample = _final_norm(xs, g_final, _row_tile(bs * ts, 512))
    stack = lambda rows, i: jnp.stack([r[i] for r in rows], axis=0)
    return ((y_prompt, y_sample) + tuple(stack(p_rows, i) for i in range(11))
            + tuple(stack(s_rows, i) for i in range(11)))
```

```python
import functools
import math

import jax
import jax.numpy as jnp
from jax import lax
from jax.experimental import pallas as pl
from jax.experimental.pallas import tpu as pltpu

F32 = jnp.float32
BF16 = jnp.bfloat16
I32 = jnp.int32
I16 = jnp.int16

N_HEADS = 4
HEAD_DIM = 64
GROUP_WIDTH = N_HEADS * HEAD_DIM
DIFF_DH = HEAD_DIM // 2
N_IDX_HEADS = 8
IDX_DIM = 64
CHUNK = 64
TOPK_MAX = 256
T5_BUCKETS = 32
T5_MAX_DIST = 128
CONV_W = 3
RMS_EPS = 1e-6
SUBLN_EPS = 1e-5

LANES = 128
SUBLANES = 8
SB_KB = 256
KB = 512
FRONT = 256
TQ_MAX = 256
N_EXTRA = 3
COL_FLAG = HEAD_DIM + N_EXTRA
V_ROWS = 80
NEG = -1e30
SB_DEAD = -110.0
HALF_BITS = 16
HALF_BIAS = 2 ** (HALF_BITS - 1)
PACKED_ROWS = 16
KEY_NEG_INF = -2139095041
LOG2E = math.log2(math.e)
VMEM_LIMIT = 56 * 1024 * 1024

COL_QIX = 9 * GROUP_WIDTH
COL_SLAB = COL_QIX + N_IDX_HEADS * IDX_DIM
SLAB_WIX = IDX_DIM
SLAB_GATE = IDX_DIM + N_IDX_HEADS
COL_FX = COL_SLAB + LANES
PROJ_PAD = COL_FX + 3 * GROUP_WIDTH


def _cparams(sem):
    return pltpu.CompilerParams(dimension_semantics=sem, vmem_limit_bytes=VMEM_LIMIT)


def _resident(shape, index_map):
    return pl.BlockSpec(shape, index_map, pipeline_mode=pl.Buffered(1))


def _full(shape):
    return _resident(shape, lambda b, i: (0,) * len(shape))


def _log_sigmoid(z):
    return jnp.minimum(z, 0.0) - jnp.log(1.0 + jnp.exp(-jnp.abs(z)))


def _dot(a, b):
    return jnp.dot(a, b, preferred_element_type=F32)


PROJ_SEGMENTS = ([(i * GROUP_WIDTH, GROUP_WIDTH) for i in range(9)]
                 + [(COL_QIX, COL_SLAB - COL_QIX), (COL_SLAB, LANES)]
                 + [(COL_FX + i * GROUP_WIDTH, GROUP_WIDTH) for i in range(3)])


def _proj_kernel(x_ref, g_ref, w_ref, bf_ref, *out_refs):
    x = x_ref[...]
    hn = x * lax.rsqrt(jnp.mean(x * x, axis=-1, keepdims=True) + RMS_EPS) * g_ref[...]
    y = _dot(hn.astype(BF16), w_ref[...])
    for (col, width), o_ref in zip(PROJ_SEGMENTS, out_refs):
        o_ref[...] = y[:, col:col + width]
    out_refs[-1][...] = _log_sigmoid(y[:, COL_SLAB:COL_SLAB + LANES] + bf_ref[...])


def _proj(x2, g, w, bf, tm):
    n, d = x2.shape
    widths = [width for _, width in PROJ_SEGMENTS] + [LANES]
    return pl.pallas_call(
        _proj_kernel,
        grid=(n // tm,),
        in_specs=[pl.BlockSpec((tm, d), lambda i: (i, 0)),
                  pl.BlockSpec((1, d), lambda i: (0, 0)),
                  _resident((d, PROJ_PAD), lambda i: (0, 0)),
                  pl.BlockSpec((1, LANES), lambda i: (0, 0))],
        out_specs=[pl.BlockSpec((tm, width), lambda i: (i, 0)) for width in widths],
        out_shape=[jax.ShapeDtypeStruct((n, width), F32) for width in widths],
        compiler_params=_cparams(("parallel",)),
        name="proj",
    )(x2, g, w, bf)


def _cumsum_kernel(x_ref, o_ref):
    nch, rows, _ = x_ref.shape
    lane = lax.broadcasted_iota(I32, (rows, LANES), 1)

    def body(c, carry):
        x = x_ref[c]
        d = 1
        while d < LANES:
            x = x + jnp.where(lane >= d, pltpu.roll(x, d, 1), 0.0)
            d *= 2
        x = x + carry
        o_ref[c] = x
        return jnp.broadcast_to(x[:, LANES - 1:LANES], (rows, LANES))

    lax.fori_loop(0, nch, body, jnp.zeros((rows, LANES), F32))


def _cumsum_seq(logf_all):
    b, l, h = logf_all.shape
    rows = b * h
    rows_pad = -(-rows // SUBLANES) * SUBLANES
    x = jnp.transpose(logf_all, (0, 2, 1)).reshape(rows, l // LANES, LANES)
    x = jnp.pad(jnp.transpose(x, (1, 0, 2)), ((0, 0), (0, rows_pad - rows), (0, 0)))
    c = pl.pallas_call(
        _cumsum_kernel,
        out_shape=jax.ShapeDtypeStruct(x.shape, F32),
        compiler_params=pltpu.CompilerParams(vmem_limit_bytes=VMEM_LIMIT),
        name="cumsum",
    )(x)
    return jnp.transpose(c[:, :rows], (1, 0, 2)).reshape(b, h, l)


def _sb_kernel(qt_ref, k_ref, vt_ref, later_ref, o_ref, run_scr, acc_scr, *, past, tq):
    q0 = pl.multiple_of(past + pl.program_id(1) * tq, SB_KB)
    run_scr[...] = jnp.zeros(run_scr.shape, F32)
    acc_scr[...] = jnp.zeros(acc_scr.shape, F32)
    later = later_ref[...]

    def block(start, strict):
        logits = [_dot(k_ref[0, h, pl.ds(start, SB_KB), :], qt_ref[0, h]) for h in range(N_HEADS)]
        log_beta, log_keep, pieces = [], [], []
        for h in range(N_HEADS):
            z = logits[h]
            ls = _log_sigmoid(z)
            lk = ls - z
            if strict is not None:
                lk = jnp.where(strict, lk, 0.0)
            hi = lk.astype(BF16)
            rem = lk - hi.astype(F32)
            mid = rem.astype(BF16)
            lo = (rem - mid.astype(F32)).astype(BF16)
            log_beta.append(ls)
            log_keep.append(lk)
            pieces.append((hi, mid, lo))
        between = [_dot(later, pieces[h][0]) + _dot(later, pieces[h][1]) + _dot(later, pieces[h][2])
                   for h in range(N_HEADS)]
        weights = []
        for h in range(N_HEADS):
            a = jnp.exp(log_beta[h] + between[h] + run_scr[h])
            if strict is not None:
                a = jnp.where(strict, a, 0.0)
            weights.append(a.astype(BF16))
        for h in range(N_HEADS):
            acc_scr[h] = acc_scr[h] + _dot(vt_ref[0, h, :, pl.ds(start, SB_KB)], weights[h])
            run_scr[h] = run_scr[h] + jnp.sum(log_keep[h], axis=0, keepdims=True)

    block(q0, lax.broadcasted_iota(I32, (SB_KB, tq), 0) < lax.broadcasted_iota(I32, (SB_KB, tq), 1))

    def cond(carry):
        return (carry[0] >= 0) & (carry[1] > SB_DEAD)

    def body(carry):
        block(pl.multiple_of(carry[0], SB_KB), None)
        return carry[0] - SB_KB, jnp.max(run_scr[...])

    lax.while_loop(cond, body, (q0 - SB_KB, jnp.max(run_scr[...])))
    for h in range(N_HEADS):
        o_ref[0, h * HEAD_DIM:(h + 1) * HEAD_DIM, :] = acc_scr[h]


def _sb_attn(qt, k, vt, later, past, tq):
    b, _, _, t = qt.shape
    l_sb = k.shape[2]
    return pl.pallas_call(
        functools.partial(_sb_kernel, past=past, tq=tq),
        grid=(b, t // tq),
        in_specs=[pl.BlockSpec((1, N_HEADS, HEAD_DIM, tq), lambda b, i: (b, 0, 0, i)),
                  _resident((1, N_HEADS, l_sb, HEAD_DIM), lambda b, i: (b, 0, 0, 0)),
                  _resident((1, N_HEADS, HEAD_DIM, l_sb), lambda b, i: (b, 0, 0, 0)),
                  _full(later.shape)],
        out_specs=pl.BlockSpec((1, GROUP_WIDTH, tq), lambda b, i: (b, 0, i)),
        out_shape=jax.ShapeDtypeStruct((b, GROUP_WIDTH, t), F32),
        scratch_shapes=[pltpu.VMEM((N_HEADS, 1, tq), F32), pltpu.VMEM((N_HEADS, HEAD_DIM, tq), F32)],
        compiler_params=_cparams(("parallel", "parallel")),
        name="stick_breaking",
    )(qt, k, vt, later)


def _n_far_blocks(q0):
    return jnp.maximum(q0 - FRONT + KB - 1, 0) // KB


def _far_start(q0, j):
    return pl.multiple_of(q0 - KB * (j + 1), FRONT)


def _flash_tile(n_maps, head_of, qa_ref, ka_ref, va_ref, state, q0, near_add, far_add):
    m_scr, acc_scr, s_scr, smax_scr = state
    m_scr[...] = jnp.full(m_scr.shape, NEG, F32)
    acc_scr[...] = jnp.zeros(acc_scr.shape, F32)

    def start_of(i):
        return pl.multiple_of(q0 - KB * i, FRONT)

    def scores(mp, start, add):
        s = _dot(ka_ref[0, head_of(mp), pl.ds(start, KB), :], qa_ref[0, mp])
        if add is not None:
            s = s + add(mp)
        s_scr[mp] = s
        smax_scr[mp] = jnp.max(s, axis=0, keepdims=True)

    def absorb(mp, start):
        m_old = m_scr[mp]
        m_new = jnp.maximum(m_old, smax_scr[mp])
        p = jnp.exp2(s_scr[mp] - m_new).astype(BF16)
        acc_scr[mp] = (acc_scr[mp] * jnp.exp2(m_old - m_new)
                       + _dot(va_ref[0, head_of(mp), :, pl.ds(start, KB)], p))
        m_scr[mp] = m_new

    for mp in range(n_maps):
        scores(mp, start_of(0), near_add)

    def body(i, c):
        nxt = start_of(i + 1)
        for mp in range(n_maps):
            absorb(mp, start_of(i))
            scores(mp, nxt, None if far_add is None else (lambda mp_: far_add(nxt)))
        return c

    n_far = _n_far_blocks(q0)
    lax.fori_loop(0, n_far, body, 0)
    for mp in range(n_maps):
        absorb(mp, start_of(n_far))
    return acc_scr


def _normalized(acc_scr, mp):
    return acc_scr[mp, 0:HEAD_DIM, :] / acc_scr[mp, HEAD_DIM:HEAD_DIM + 1, :]


def _flash_specs(n_maps, tq, lp):
    qa_spec = pl.BlockSpec((1, n_maps, LANES, tq), lambda b, i: (b, 0, 0, i))
    ka_spec = _resident((1, N_HEADS, lp, LANES), lambda b, i: (b, 0, 0, 0))
    va_spec = _resident((1, N_HEADS, V_ROWS, lp), lambda b, i: (b, 0, 0, 0))
    o_spec = pl.BlockSpec((1, GROUP_WIDTH, tq), lambda b, i: (b, 0, i))
    state = [pltpu.VMEM((n_maps, 1, tq), F32), pltpu.VMEM((n_maps, V_ROWS, tq), F32),
             pltpu.VMEM((n_maps, KB, tq), F32), pltpu.VMEM((n_maps, 1, tq), F32)]
    return qa_spec, ka_spec, va_spec, o_spec, state


def _fx_kernel(qa_ref, ka_ref, va_ref, causal_ref, o_ref, *state, past, tq):
    q0 = past + pl.program_id(1) * tq
    acc_scr = _flash_tile(N_HEADS, lambda mp: mp, qa_ref, ka_ref, va_ref, state, q0,
                          lambda mp: causal_ref[...], None)
    for h in range(N_HEADS):
        o_ref[0, h * HEAD_DIM:(h + 1) * HEAD_DIM, :] = _normalized(acc_scr, h)


def _fx_attn(qa, ka, va, causal, past, tq):
    b, _, _, t = qa.shape
    qa_spec, ka_spec, va_spec, o_spec, scratch = _flash_specs(N_HEADS, tq, ka.shape[2])
    return pl.pallas_call(
        functools.partial(_fx_kernel, past=past, tq=tq),
        grid=(b, t // tq),
        in_specs=[qa_spec, ka_spec, va_spec, _full(causal.shape)],
        out_specs=o_spec,
        out_shape=jax.ShapeDtypeStruct((b, GROUP_WIDTH, t), F32),
        scratch_shapes=scratch,
        compiler_params=_cparams(("parallel", "parallel")),
        name="forgetting",
    )(qa, ka, va, causal)


def _df_kernel(qa_ref, ka_ref, va_ref, near_ref, lam_ref, g_ref, o_ref, *state, past, tq):
    q0 = past + pl.program_id(1) * tq
    acc_scr = _flash_tile(2 * N_HEADS, lambda mp: mp // 2, qa_ref, ka_ref, va_ref, state, q0,
                          lambda mp: near_ref[mp // 2], None)
    lam_init = lam_ref[4:5, 0:1]
    lam = (jnp.exp(jnp.sum(lam_ref[0:1, :] * lam_ref[1:2, :], axis=-1, keepdims=True))
           - jnp.exp(jnp.sum(lam_ref[2:3, :] * lam_ref[3:4, :], axis=-1, keepdims=True)) + lam_init)
    for h in range(N_HEADS):
        o = _normalized(acc_scr, 2 * h) - lam * _normalized(acc_scr, 2 * h + 1)
        o = o * lax.rsqrt(jnp.mean(o * o, axis=0, keepdims=True) + SUBLN_EPS) * g_ref[...]
        o_ref[0, h * HEAD_DIM:(h + 1) * HEAD_DIM, :] = o * (1.0 - lam_init)


def _df_attn(qa, ka, va, near, lam_rows, g_cols, past, tq):
    b, _, _, t = qa.shape
    qa_spec, ka_spec, va_spec, o_spec, scratch = _flash_specs(2 * N_HEADS, tq, ka.shape[2])
    return pl.pallas_call(
        functools.partial(_df_kernel, past=past, tq=tq),
        grid=(b, t // tq),
        in_specs=[qa_spec, ka_spec, va_spec, _full(near.shape), _full(lam_rows.shape), _full(g_cols.shape)],
        out_specs=o_spec,
        out_shape=jax.ShapeDtypeStruct((b, GROUP_WIDTH, t), F32),
        scratch_shapes=scratch,
        compiler_params=_cparams(("parallel", "parallel")),
        name="differential",
    )(qa, ka, va, near, lam_rows, g_cols)


def _ds_kernel(qix_ref, wix_ref, kix_ref, qa_ref, ka_ref, va_ref, near_ref, upto_ref, o_ref,
               key_scr, k16_scr, msk_scr, *state, past, tq, l_keys, topk):
    q0 = past + pl.program_id(1) * tq
    near0 = pl.multiple_of(q0, FRONT)
    n_far = _n_far_blocks(q0)
    wix = wix_ref[0] * (N_IDX_HEADS ** -0.5)

    def score_block(start, near):
        kix = kix_ref[0, pl.ds(start, KB), :]
        sc = jnp.zeros((KB, tq), F32)
        for hh in range(N_IDX_HEADS):
            sc = sc + wix[hh:hh + 1, :] * jnp.maximum(_dot(kix, qix_ref[0, hh]), 0.0)
        kidx = start + lax.broadcasted_iota(I32, (KB, tq), 0)
        ok = (kidx >= FRONT) & (kidx < FRONT + l_keys)
        if near:
            ok = ok & (near_ref[0] > 0.5 * NEG)
        bits = lax.bitcast_convert_type(jnp.where(ok, sc, -jnp.inf), I32)
        key = bits ^ ((bits >> 31) & 0x7FFFFFFF)
        key_scr[pl.ds(start, KB), :] = key
        k16_scr[pl.ds(start, KB), :] = (key >> HALF_BITS).astype(I16)

    def each_block(fn):
        fn(near0)

        def far(j, c):
            fn(_far_start(q0, j))
            return c

        lax.fori_loop(0, n_far, far, 0)

    score_block(near0, True)

    def score_far(j, c):
        score_block(_far_start(q0, j), False)
        return c

    lax.fori_loop(0, n_far, score_far, 0)

    def count16(cand):
        cand16 = cand.astype(I16)

        def blk(start):
            ones = jnp.where(k16_scr[pl.ds(start, KB), :] >= cand16, jnp.ones((), BF16), jnp.zeros((), BF16))
            parts = [ones[r:r + PACKED_ROWS] for r in range(0, KB, PACKED_ROWS)]
            while len(parts) > 1:
                parts = [parts[r] + parts[r + 1] for r in range(0, len(parts), 2)]
            return parts[0].astype(F32)

        acc = lax.fori_loop(0, n_far, lambda j, c: c + blk(_far_start(q0, j)), blk(near0))
        return jnp.sum(acc, axis=0, keepdims=True)

    def greedy16():
        def bit_body(i, u):
            cand = u | lax.shift_left(jnp.int32(1), HALF_BITS - 1 - i)
            return jnp.where(count16(cand - HALF_BIAS) >= topk, cand, u)
        return lax.fori_loop(0, HALF_BITS, bit_body, jnp.zeros((1, tq), I32))

    thr_hi = greedy16() - HALF_BIAS

    def refine_block(start):
        key = key_scr[pl.ds(start, KB), :]
        hi = key >> HALF_BITS
        low = jnp.where(hi > thr_hi, HALF_BIAS - 1,
                        jnp.where(hi < thr_hi, -HALF_BIAS, (key & (2 * HALF_BIAS - 1)) - HALF_BIAS))
        k16_scr[pl.ds(start, KB), :] = low.astype(I16)

    each_block(refine_block)
    thr = thr_hi * (2 * HALF_BIAS) + greedy16()

    def count2(start):
        key = key_scr[pl.ds(start, KB), :]
        return (jnp.sum((key >= thr).astype(I32), axis=0, keepdims=True),
                jnp.sum((key > thr).astype(I32), axis=0, keepdims=True))

    def count2_far(j, c):
        ge, gt = count2(_far_start(q0, j))
        return c[0] + ge, c[1] + gt

    n_ge, n_gt = lax.fori_loop(0, n_far, count2_far, count2(near0))
    tied = jnp.max(n_ge) > topk

    @pl.when(jnp.logical_not(tied))
    def _():
        def plain_block(start):
            key = key_scr[pl.ds(start, KB), :]
            msk_scr[pl.ds(start, KB), :] = jnp.where((key >= thr) & (key > KEY_NEG_INF), 0.0, NEG)
        each_block(plain_block)

    @pl.when(tied)
    def _():
        need = (topk - n_gt).astype(F32)

        def mask_block(start, seen):
            key = key_scr[pl.ds(start, KB), :]
            tie = key == thr
            rank = _dot(upto_ref[...], tie.astype(BF16)) + seen
            sel = (key > KEY_NEG_INF) & ((key > thr) | (tie & (rank <= need)))
            msk_scr[pl.ds(start, KB), :] = jnp.where(sel, 0.0, NEG)
            return seen + jnp.sum(tie.astype(F32), axis=0, keepdims=True)

        seen = lax.fori_loop(0, n_far, lambda i, c: mask_block(_far_start(q0, n_far - 1 - i), c),
                             jnp.zeros((1, tq), F32))
        mask_block(near0, seen)

    acc_scr = _flash_tile(N_HEADS, lambda mp: mp, qa_ref, ka_ref, va_ref, state, q0,
                          lambda mp: near_ref[mp] + msk_scr[pl.ds(near0, KB), :],
                          lambda start: msk_scr[pl.ds(start, KB), :])
    for h in range(N_HEADS):
        o_ref[0, h * HEAD_DIM:(h + 1) * HEAD_DIM, :] = _normalized(acc_scr, h)


def _ds_attn(qix, wix, kix, qa, ka, va, near, upto, past, tq, l_keys, topk):
    b, _, _, t = qa.shape
    lp = ka.shape[2]
    qa_spec, ka_spec, va_spec, o_spec, scratch = _flash_specs(N_HEADS, tq, lp)
    return pl.pallas_call(
        functools.partial(_ds_kernel, past=past, tq=tq, l_keys=l_keys, topk=topk),
        grid=(b, t // tq),
        in_specs=[pl.BlockSpec((1, N_IDX_HEADS, IDX_DIM, tq), lambda b, i: (b, 0, 0, i)),
                  pl.BlockSpec((1, N_IDX_HEADS, tq), lambda b, i: (b, 0, i)),
                  _resident((1, lp, IDX_DIM), lambda b, i: (b, 0, 0)),
                  qa_spec, ka_spec, va_spec, _full(near.shape), _full(upto.shape)],
        out_specs=o_spec,
        out_shape=jax.ShapeDtypeStruct((b, GROUP_WIDTH, t), F32),
        scratch_shapes=[pltpu.VMEM((lp, tq), I32), pltpu.VMEM((lp, tq), I16), pltpu.VMEM((lp, tq), F32)] + scratch,
        compiler_params=_cparams(("parallel", "parallel")),
        name="indexed_sparse",
    )(qix, wix, kix, qa, ka, va, near, upto)


def _post_kernel(x_ref, o0_ref, o1_ref, o2_ref, o3_ref, wo_ref, g_ref, wg_ref, wu_ref, wd_ref,
                 cw_ref, cb_ref, pc_ref, xo_ref, st_ref, uext, *, tm):
    t = pl.program_id(1)

    @pl.when(t == 0)
    def _():
        uext[0:SUBLANES, :] = pc_ref[0]

    @pl.when(t > 0)
    def _():
        uext[0:SUBLANES, :] = uext[tm:tm + SUBLANES, :]

    x = x_ref[0]
    for gi, ot_ref in enumerate((o0_ref, o1_ref, o2_ref, o3_ref)):
        x = x + lax.dot_general(ot_ref[0].astype(BF16), wo_ref[gi * GROUP_WIDTH:(gi + 1) * GROUP_WIDTH, :],
                                (((0,), (0,)), ((), ())), preferred_element_type=F32)
    hn = (x * lax.rsqrt(jnp.mean(x * x, axis=-1, keepdims=True) + RMS_EPS) * g_ref[...]).astype(BF16)
    u = _dot(hn, wg_ref[...])
    uext[SUBLANES:SUBLANES + tm, :] = u
    conv = cb_ref[...]
    conv = conv + cw_ref[0:1, :] * uext[SUBLANES - 2:SUBLANES - 2 + tm, :]
    conv = conv + cw_ref[1:2, :] * uext[SUBLANES - 1:SUBLANES - 1 + tm, :]
    conv = conv + cw_ref[2:3, :] * u
    hidden = conv * (1.0 / (1.0 + jnp.exp(-conv))) * _dot(hn, wu_ref[...])
    xo_ref[0] = x + _dot(hidden.astype(BF16), wd_ref[...])
    st_ref[0] = uext[tm:tm + SUBLANES, :]


def _post(x, outs, wo, g, wg, wu, wd, cw, cb, pconv, tm):
    b, t, d = x.shape
    f = wg.shape[1]
    row = lambda w: pl.BlockSpec((1, tm, w), lambda bi, ti: (bi, ti, 0))
    const = lambda shape: _resident(shape, lambda bi, ti: (0,) * len(shape))
    return pl.pallas_call(
        functools.partial(_post_kernel, tm=tm),
        grid=(b, t // tm),
        in_specs=[row(d)] + [pl.BlockSpec((1, GROUP_WIDTH, tm), lambda bi, ti: (bi, 0, ti))] * 4 + [
            const(wo.shape), const(g.shape), const(wg.shape), const(wu.shape), const(wd.shape),
            const(cw.shape), const(cb.shape),
            pl.BlockSpec((1, SUBLANES, f), lambda bi, ti: (bi, 0, 0))],
        out_specs=[row(d), pl.BlockSpec((1, SUBLANES, f), lambda bi, ti: (bi, 0, 0))],
        out_shape=[jax.ShapeDtypeStruct((b, t, d), F32),
                   jax.ShapeDtypeStruct((b, SUBLANES, f), F32)],
        scratch_shapes=[pltpu.VMEM((tm + SUBLANES, f), F32)],
        compiler_params=_cparams(("parallel", "arbitrary")),
        name="out_proj_mlp",
    )(x, *outs, wo, g, wg, wu, wd, cw, cb, pconv)


def _final_norm_kernel(x_ref, g_ref, o_ref):
    x = x_ref[...]
    o_ref[...] = x * lax.rsqrt(jnp.mean(x * x, axis=-1, keepdims=True) + RMS_EPS) * g_ref[...]


def _final_norm(x, g, tm):
    b, t, d = x.shape
    x2 = x.reshape(b * t, d)
    y = pl.pallas_call(
        _final_norm_kernel,
        grid=(b * t // tm,),
        in_specs=[pl.BlockSpec((tm, d), lambda i: (i, 0)), pl.BlockSpec((1, d), lambda i: (0, 0))],
        out_specs=pl.BlockSpec((tm, d), lambda i: (i, 0)),
        out_shape=jax.ShapeDtypeStruct(x2.shape, F32),
        compiler_params=_cparams(("parallel",)),
        name="final_norm",
    )(x2, g.reshape(1, d))
    return y.reshape(b, t, d)


def _t5_bucket(rel):
    half = T5_BUCKETS // 2
    max_exact = half // 2
    base = jnp.where(rel > 0, half, 0)
    n = jnp.abs(rel)
    nf = jnp.maximum(n, 1).astype(F32)
    large = max_exact + (jnp.log(nf / max_exact) / math.log(T5_MAX_DIST / max_exact)
                         * (half - max_exact)).astype(I32)
    large = jnp.minimum(large, half - 1)
    return base + jnp.where(n < max_exact, n, large)


def _bf16_part(x):
    bits = lax.bitcast_convert_type(x, jnp.uint32) & jnp.uint32(0xFFFF0000)
    return lax.bitcast_convert_type(bits, F32)


def _split3(x):
    hi = _bf16_part(x)
    mid = _bf16_part(x - hi)
    lo = _bf16_part(x - hi - mid)
    return jnp.stack([hi, mid, lo], axis=-1)


def _near_tables(t5_table, tq):
    kk = jnp.arange(KB, dtype=I32)[:, None] - FRONT
    qq = jnp.arange(tq, dtype=I32)[None, :]
    tab = t5_table.astype(F32).T
    far = tab[:, _t5_bucket(jnp.int32(-T5_MAX_DIST))]
    one_hot = (_t5_bucket(kk - qq)[:, :, None] == jnp.arange(T5_BUCKETS, dtype=I32)).astype(F32)
    bias = jnp.einsum("kqb,hb->hkq", one_hot, tab, precision=lax.Precision.HIGHEST)
    chunk_ok = (kk // CHUNK) <= (qq // CHUNK)
    near = jnp.where(chunk_ok[None], (bias - far[:, None, None]) * LOG2E, NEG)
    causal = jnp.where(kk <= qq, 0.0, NEG).astype(F32)
    return near, far * LOG2E, causal


def _pack_w_in(w_in):
    d = w_in.shape[0]
    fx0 = COL_SLAB + IDX_DIM + N_IDX_HEADS
    pad = jnp.zeros((d, LANES - SLAB_GATE - N_HEADS), w_in.dtype)
    return jnp.concatenate([w_in[:, :fx0], w_in[:, fx0 + 3 * GROUP_WIDTH:], pad,
                            w_in[:, fx0:fx0 + 3 * GROUP_WIDTH]], axis=1).astype(BF16)


def _with_past(past, new, l_pad):
    b, t = new.shape[:2]
    new = new.reshape(b, t, -1)
    if past is not None:
        new = jnp.concatenate([past.reshape(b, past.shape[1], -1).astype(F32), new], axis=1)
    return jnp.pad(new, ((0, 0), (0, l_pad - new.shape[1]), (0, 0)))


def _aug_queries(q, rows):
    b, t, m, _ = q.shape
    extra = jnp.zeros((m, LANES - HEAD_DIM), F32).at[:, :N_EXTRA].set(rows).at[:, N_EXTRA].set(1.0)
    extra = jnp.broadcast_to(extra[None, :, :, None], (b, m, LANES - HEAD_DIM, t))
    return jnp.concatenate([jnp.transpose(q, (0, 2, 3, 1)), extra], axis=2).astype(BF16)


def _aug_keys(k, cols, lp):
    b, l, h, _ = k.shape
    body = jnp.concatenate([k, cols, jnp.zeros((b, l, h, LANES - COL_FLAG), F32)], axis=-1)
    pad_row = jnp.zeros((LANES,), F32).at[COL_FLAG].set(NEG)
    pad = lambda n: jnp.broadcast_to(pad_row, (b, n, h, LANES))
    full = jnp.concatenate([pad(FRONT), body, pad(lp - FRONT - l)], axis=1)
    return jnp.transpose(full, (0, 2, 1, 3)).astype(BF16)


def _aug_values(v, lp):
    b, l, h, _ = v.shape
    vt = jnp.transpose(v, (0, 2, 3, 1))
    body = jnp.concatenate([vt, jnp.ones((b, h, 1, l), F32),
                            jnp.zeros((b, h, V_ROWS - HEAD_DIM - 1, l), F32)], axis=2)
    return jnp.pad(body, ((0, 0), (0, 0), (0, 0), (FRONT, lp - FRONT - l))).astype(BF16)


def _layer(x, past, lw, lam_init, t5_table, tm_proj, tm_post):
    (g_mix, w_in_p, b_forget, w_out, lam_q1, lam_k1, lam_q2, lam_k2, diff_subln,
     g_ffn, w_gate, w_up, w_down, conv_w, conv_b) = lw
    b, t, d = x.shape
    p_len = 0 if past is None else past[0].shape[1]
    l_keys = p_len + t
    l_pad = -(-l_keys // LANES) * LANES
    topk = min(TOPK_MAX, l_keys // 4)
    tq = min(t, TQ_MAX)
    assert p_len % FRONT == 0 and (t % TQ_MAX == 0 or t <= LANES) and FRONT == SB_KB == TQ_MAX
    lp = -(-max(FRONT + l_keys, p_len + t - tq + KB) // LANES) * LANES
    l_sb = p_len + t - tq + SB_KB
    (pk_sb, pv_sb, pk_df, pv_df, pk_ds, pv_ds, pk_ix, pk_fx, pv_fx, plogf, pconv) = (
        past if past is not None else (None,) * 11)

    bf = jnp.zeros((1, LANES), F32).at[0, SLAB_GATE:SLAB_GATE + N_HEADS].set(b_forget)
    parts = _proj(x.reshape(b * t, d), g_mix.reshape(1, d), w_in_p, bf, tm_proj)
    (q_sb, k_sb, v_sb, q_df, k_df, v_df, q_ds, k_ds, v_ds, q_ix, slab, q_fx, k_fx, v_fx, lf) = [
        a.reshape(b, t, -1) for a in parts]
    k_ix = slab[:, :, :IDX_DIM]
    w_ix = slab[:, :, SLAB_WIX:SLAB_WIX + N_IDX_HEADS]
    logf = lf[:, :, SLAB_GATE:SLAB_GATE + N_HEADS]

    scale = HEAD_DIM ** -0.5
    hd = lambda a: a.reshape(b, -1, N_HEADS, HEAD_DIM)
    all_keys = lambda pst, new: hd(_with_past(pst, new, l_keys))
    near, far, causal = _near_tables(t5_table, tq)
    ones_cols = jnp.ones((b, l_keys, N_HEADS, N_EXTRA), F32)

    k_sb_all = jnp.transpose(hd(_with_past(pk_sb, k_sb, l_sb)), (0, 2, 1, 3)).astype(BF16)
    vt_sb_all = jnp.transpose(hd(_with_past(pv_sb, v_sb, l_sb)), (0, 2, 3, 1)).astype(BF16)
    later = (jnp.arange(SB_KB)[None, :] > jnp.arange(SB_KB)[:, None]).astype(BF16)
    o_sb = _sb_attn(jnp.transpose(hd(q_sb * scale), (0, 2, 3, 1)).astype(BF16), k_sb_all, vt_sb_all, later,
                    p_len, tq)

    half = jnp.arange(HEAD_DIM) < DIFF_DH
    q_df2 = jnp.where(jnp.stack([half, ~half])[None, None, None],
                      q_df.reshape(b, t, N_HEADS, 1, HEAD_DIM) * (DIFF_DH ** -0.5 * LOG2E), 0.0)
    lam_rows = jnp.zeros((SUBLANES, LANES), F32)
    for r, vec in enumerate((lam_q1, lam_k1, lam_q2, lam_k2)):
        lam_rows = lam_rows.at[r, :DIFF_DH].set(vec.astype(F32))
    lam_rows = lam_rows.at[4, :].set(lam_init)
    g_cols = jnp.broadcast_to(diff_subln.astype(F32)[:, None], (HEAD_DIM, tq))
    o_df = _df_attn(_aug_queries(q_df2.reshape(b, t, 2 * N_HEADS, HEAD_DIM),
                                 jnp.repeat(_split3(far[:N_HEADS]), 2, axis=0)),
                    _aug_keys(all_keys(pk_df, k_df), ones_cols, lp), _aug_values(all_keys(pv_df, v_df), lp),
                    near[:N_HEADS], lam_rows, g_cols, p_len, tq)

    kix_all = jnp.pad(_with_past(pk_ix, k_ix, l_keys), ((0, 0), (FRONT, lp - FRONT - l_keys), (0, 0))).astype(BF16)
    qix_t = jnp.transpose(q_ix.reshape(b, t, N_IDX_HEADS, IDX_DIM) * (IDX_DIM ** -0.5), (0, 2, 3, 1)).astype(BF16)
    upto = (jnp.arange(KB)[None, :] <= jnp.arange(KB)[:, None]).astype(BF16)
    o_ds = _ds_attn(qix_t, jnp.transpose(w_ix, (0, 2, 1)), kix_all,
                    _aug_queries(hd(q_ds * (scale * LOG2E)), _split3(far[N_HEADS:])),
                    _aug_keys(all_keys(pk_ds, k_ds), ones_cols, lp), _aug_values(all_keys(pv_ds, v_ds), lp),
                    near[N_HEADS:], upto, p_len, tq, l_keys, topk)

    c_all = _cumsum_seq(_with_past(plogf, logf, l_pad))
    c_keys = jnp.transpose(c_all[:, :, :l_keys], (0, 2, 1))
    o_fx = _fx_attn(_aug_queries(hd(q_fx * (scale * LOG2E)), jnp.ones((N_HEADS, N_EXTRA), F32)),
                    _aug_keys(all_keys(pk_fx, k_fx), _split3(-LOG2E * c_keys), lp),
                    _aug_values(all_keys(pv_fx, v_fx), lp), causal, p_len, tq)

    f = w_gate.shape[1]
    if pconv is None:
        pc = jnp.zeros((b, SUBLANES, f), F32)
    else:
        pc = jnp.pad(pconv.astype(F32), ((0, 0), (SUBLANES - (CONV_W - 1), 0), (0, 0)))
    cw = jnp.pad(conv_w.astype(F32), ((0, SUBLANES - CONV_W), (0, 0)))
    x_new, st = _post(x, (o_sb, o_df, o_ds, o_fx), w_out, g_ffn.reshape(1, d), w_gate, w_up,
                      w_down, cw, conv_b.reshape(1, f).astype(F32), pc, tm_post)

    rows = (hd(k_sb), hd(v_sb), k_df.reshape(b, t, N_HEADS, 2, DIFF_DH), hd(v_df), hd(k_ds), hd(v_ds),
            k_ix, hd(k_fx), hd(v_fx), logf, st[:, SUBLANES - (CONV_W - 1):])
    return x_new, rows


def _row_tile(n, cap):
    tm = min(n, cap)
    assert n % tm == 0
    return tm


def kernel(x_prompt, x_sample, cache_sb_k, cache_sb_v, cache_diff_k, cache_diff_v, cache_dsa_k, cache_dsa_v, cache_dsa_kidx, cache_fox_k, cache_fox_v, cache_fox_logf, state_ffn_conv, g_mix, w_in, b_forget, w_out, lam_q1, lam_k1, lam_q2, lam_k2, diff_subln, g_ffn, w_gate, w_up, w_down, conv_w, conv_b, t5_table, g_final):
    depth = w_in.shape[0]
    bp, tp, _ = x_prompt.shape
    bs, ts, _ = x_sample.shape
    xp, xs = x_prompt, x_sample
    p_rows, s_rows = [], []
    for l in range(depth):
        lw = (g_mix[l], _pack_w_in(w_in[l]), b_forget[l], w_out[l].astype(BF16), lam_q1[l], lam_k1[l],
              lam_q2[l], lam_k2[l], diff_subln[l], g_ffn[l], w_gate[l].astype(BF16), w_up[l].astype(BF16),
              w_down[l].astype(BF16), conv_w[l], conv_b[l])
        lam_init = 0.8 - 0.6 * math.exp(-0.3 * l)
        xp, rp = _layer(xp, None, lw, lam_init, t5_table, _row_tile(bp * tp, 512), _row_tile(tp, 256))
        s_past = (cache_sb_k[l], cache_sb_v[l], cache_diff_k[l], cache_diff_v[l], cache_dsa_k[l],
                  cache_dsa_v[l], cache_dsa_kidx[l], cache_fox_k[l], cache_fox_v[l], cache_fox_logf[l],
                  state_ffn_conv[l])
        xs, rs = _layer(xs, s_past, lw, lam_init, t5_table, _row_tile(bs * ts, 512), _row_tile(ts, 256))
        p_rows.append(rp)
        s_rows.append(rs)
    y_prompt = _final_norm(xp, g_final, _row_tile(bp * tp, 512))
    y_sample = _final_norm(xs, g_final, _row_tile(bs * ts, 512))
    stack = lambda rows, i: jnp.stack([r[i] for r in rows], axis=0)
    return ((y_prompt, y_sample) + tuple(stack(p_rows, i) for i in range(11))
            + tuple(stack(s_rows, i) for i in range(11)))
```

```python
import functools
import math

import jax
import jax.numpy as jnp
from jax import lax
from jax.experimental import pallas as pl
from jax.experimental.pallas import tpu as pltpu

F32 = jnp.float32
BF16 = jnp.bfloat16
I32 = jnp.int32
I16 = jnp.int16

N_HEADS = 4
HEAD_DIM = 64
GROUP_WIDTH = N_HEADS * HEAD_DIM
DIFF_DH = HEAD_DIM // 2
N_IDX_HEADS = 8
IDX_DIM = 64
CHUNK = 64
TOPK_MAX = 256
T5_BUCKETS = 32
T5_MAX_DIST = 128
CONV_W = 3
RMS_EPS = 1e-6
SUBLN_EPS = 1e-5

LANES = 128
SUBLANES = 8
SB_KB = 256
KB = 512
FRONT = 256
TQ_MAX = 256
N_EXTRA = 3
COL_FLAG = HEAD_DIM + N_EXTRA
V_ROWS = 80
NEG = -1e30
SB_DEAD = -110.0
HALF_BITS = 16
HALF_BIAS = 2 ** (HALF_BITS - 1)
PACKED_ROWS = 16
KEY_NEG_INF = -2139095041
LOG2E = math.log2(math.e)
VMEM_LIMIT = 56 * 1024 * 1024

COL_QIX = 9 * GROUP_WIDTH
COL_SLAB = COL_QIX + N_IDX_HEADS * IDX_DIM
SLAB_WIX = IDX_DIM
SLAB_GATE = IDX_DIM + N_IDX_HEADS
COL_FX = COL_SLAB + LANES
PROJ_PAD = COL_FX + 3 * GROUP_WIDTH


def _cparams(sem):
    return pltpu.CompilerParams(dimension_semantics=sem, vmem_limit_bytes=VMEM_LIMIT)


def _resident(shape, index_map):
    return pl.BlockSpec(shape, index_map, pipeline_mode=pl.Buffered(1))


def _full(shape):
    return _resident(shape, lambda b, i: (0,) * len(shape))


def _log_sigmoid(z):
    return jnp.minimum(z, 0.0) - jnp.log(1.0 + jnp.exp(-jnp.abs(z)))


def _dot(a, b):
    return jnp.dot(a, b, preferred_element_type=F32)


def _proj_kernel(x_ref, g_ref, w_ref, bf_ref, y_ref, lf_ref):
    x = x_ref[...]
    hn = x * lax.rsqrt(jnp.mean(x * x, axis=-1, keepdims=True) + RMS_EPS) * g_ref[...]
    y = _dot(hn.astype(BF16), w_ref[...])
    y_ref[...] = y
    lf_ref[...] = _log_sigmoid(y[:, COL_SLAB:COL_SLAB + LANES] + bf_ref[...])


def _proj(x2, g, w, bf, tm):
    n, d = x2.shape
    return pl.pallas_call(
        _proj_kernel,
        grid=(n // tm,),
        in_specs=[pl.BlockSpec((tm, d), lambda i: (i, 0)),
                  pl.BlockSpec((1, d), lambda i: (0, 0)),
                  _resident((d, PROJ_PAD), lambda i: (0, 0)),
                  pl.BlockSpec((1, LANES), lambda i: (0, 0))],
        out_specs=[pl.BlockSpec((tm, PROJ_PAD), lambda i: (i, 0)),
                   pl.BlockSpec((tm, LANES), lambda i: (i, 0))],
        out_shape=[jax.ShapeDtypeStruct((n, PROJ_PAD), F32),
                   jax.ShapeDtypeStruct((n, LANES), F32)],
        compiler_params=_cparams(("parallel",)),
        name="proj",
    )(x2, g, w, bf)


def _cumsum_kernel(x_ref, o_ref):
    nch, rows, _ = x_ref.shape
    lane = lax.broadcasted_iota(I32, (rows, LANES), 1)

    def body(c, carry):
        x = x_ref[c]
        d = 1
        while d < LANES:
            x = x + jnp.where(lane >= d, pltpu.roll(x, d, 1), 0.0)
            d *= 2
        x = x + carry
        o_ref[c] = x
        return jnp.broadcast_to(x[:, LANES - 1:LANES], (rows, LANES))

    lax.fori_loop(0, nch, body, jnp.zeros((rows, LANES), F32))


def _cumsum_seq(logf_all):
    b, l, h = logf_all.shape
    rows = b * h
    rows_pad = -(-rows // SUBLANES) * SUBLANES
    x = jnp.transpose(logf_all, (0, 2, 1)).reshape(rows, l // LANES, LANES)
    x = jnp.pad(jnp.transpose(x, (1, 0, 2)), ((0, 0), (0, rows_pad - rows), (0, 0)))
    c = pl.pallas_call(
        _cumsum_kernel,
        out_shape=jax.ShapeDtypeStruct(x.shape, F32),
        compiler_params=pltpu.CompilerParams(vmem_limit_bytes=VMEM_LIMIT),
        name="cumsum",
    )(x)
    return jnp.transpose(c[:, :rows], (1, 0, 2)).reshape(b, h, l)


def _sb_kernel(qt_ref, k_ref, vt_ref, later_ref, o_ref, run_scr, acc_scr, *, past, tq):
    q0 = pl.multiple_of(past + pl.program_id(1) * tq, SB_KB)
    run_scr[...] = jnp.zeros(run_scr.shape, F32)
    acc_scr[...] = jnp.zeros(acc_scr.shape, F32)
    later = later_ref[...]

    def block(start, strict):
        logits = [_dot(k_ref[0, h, pl.ds(start, SB_KB), :], qt_ref[0, h]) for h in range(N_HEADS)]
        log_beta, log_keep, pieces = [], [], []
        for h in range(N_HEADS):
            z = logits[h]
            ls = _log_sigmoid(z)
            lk = ls - z
            if strict is not None:
                lk = jnp.where(strict, lk, 0.0)
            hi = lk.astype(BF16)
            rem = lk - hi.astype(F32)
            mid = rem.astype(BF16)
            lo = (rem - mid.astype(F32)).astype(BF16)
            log_beta.append(ls)
            log_keep.append(lk)
            pieces.append((hi, mid, lo))
        between = [_dot(later, pieces[h][0]) + _dot(later, pieces[h][1]) + _dot(later, pieces[h][2])
                   for h in range(N_HEADS)]
        weights = []
        for h in range(N_HEADS):
            a = jnp.exp(log_beta[h] + between[h] + run_scr[h])
            if strict is not None:
                a = jnp.where(strict, a, 0.0)
            weights.append(a.astype(BF16))
        for h in range(N_HEADS):
            acc_scr[h] = acc_scr[h] + _dot(vt_ref[0, h, :, pl.ds(start, SB_KB)], weights[h])
            run_scr[h] = run_scr[h] + jnp.sum(log_keep[h], axis=0, keepdims=True)

    block(q0, lax.broadcasted_iota(I32, (SB_KB, tq), 0) < lax.broadcasted_iota(I32, (SB_KB, tq), 1))

    def cond(carry):
        return (carry[0] >= 0) & (carry[1] > SB_DEAD)

    def body(carry):
        block(pl.multiple_of(carry[0], SB_KB), None)
        return carry[0] - SB_KB, jnp.max(run_scr[...])

    lax.while_loop(cond, body, (q0 - SB_KB, jnp.max(run_scr[...])))
    for h in range(N_HEADS):
        o_ref[0, h * HEAD_DIM:(h + 1) * HEAD_DIM, :] = acc_scr[h]


def _sb_attn(qt, k, vt, later, past, tq):
    b, _, _, t = qt.shape
    l_sb = k.shape[2]
    return pl.pallas_call(
        functools.partial(_sb_kernel, past=past, tq=tq),
        grid=(b, t // tq),
        in_specs=[pl.BlockSpec((1, N_HEADS, HEAD_DIM, tq), lambda b, i: (b, 0, 0, i)),
                  _resident((1, N_HEADS, l_sb, HEAD_DIM), lambda b, i: (b, 0, 0, 0)),
                  _resident((1, N_HEADS, HEAD_DIM, l_sb), lambda b, i: (b, 0, 0, 0)),
                  _full(later.shape)],
        out_specs=pl.BlockSpec((1, GROUP_WIDTH, tq), lambda b, i: (b, 0, i)),
        out_shape=jax.ShapeDtypeStruct((b, GROUP_WIDTH, t), F32),
        scratch_shapes=[pltpu.VMEM((N_HEADS, 1, tq), F32), pltpu.VMEM((N_HEADS, HEAD_DIM, tq), F32)],
        compiler_params=_cparams(("parallel", "parallel")),
        name="stick_breaking",
    )(qt, k, vt, later)


def _n_far_blocks(q0):
    return jnp.maximum(q0 - FRONT + KB - 1, 0) // KB


def _far_start(q0, j):
    return pl.multiple_of(q0 - KB * (j + 1), FRONT)


def _flash_tile(n_maps, head_of, qa_ref, ka_ref, va_ref, state, q0, near_add, far_add):
    m_scr, acc_scr, s_scr, smax_scr = state
    m_scr[...] = jnp.full(m_scr.shape, NEG, F32)
    acc_scr[...] = jnp.zeros(acc_scr.shape, F32)

    def start_of(i):
        return pl.multiple_of(q0 - KB * i, FRONT)

    def scores(mp, start, add):
        s = _dot(ka_ref[0, head_of(mp), pl.ds(start, KB), :], qa_ref[0, mp])
        if add is not None:
            s = s + add(mp)
        s_scr[mp] = s
        smax_scr[mp] = jnp.max(s, axis=0, keepdims=True)

    def absorb(mp, start):
        m_old = m_scr[mp]
        m_new = jnp.maximum(m_old, smax_scr[mp])
        p = jnp.exp2(s_scr[mp] - m_new).astype(BF16)
        acc_scr[mp] = (acc_scr[mp] * jnp.exp2(m_old - m_new)
                       + _dot(va_ref[0, head_of(mp), :, pl.ds(start, KB)], p))
        m_scr[mp] = m_new

    for mp in range(n_maps):
        scores(mp, start_of(0), near_add)

    def body(i, c):
        nxt = start_of(i + 1)
        for mp in range(n_maps):
            absorb(mp, start_of(i))
            scores(mp, nxt, None if far_add is None else (lambda mp_: far_add(nxt)))
        return c

    n_far = _n_far_blocks(q0)
    lax.fori_loop(0, n_far, body, 0)
    for mp in range(n_maps):
        absorb(mp, start_of(n_far))
    return acc_scr


def _normalized(acc_scr, mp):
    return acc_scr[mp, 0:HEAD_DIM, :] / acc_scr[mp, HEAD_DIM:HEAD_DIM + 1, :]


def _flash_specs(n_maps, tq, lp):
    qa_spec = pl.BlockSpec((1, n_maps, LANES, tq), lambda b, i: (b, 0, 0, i))
    ka_spec = _resident((1, N_HEADS, lp, LANES), lambda b, i: (b, 0, 0, 0))
    va_spec = _resident((1, N_HEADS, V_ROWS, lp), lambda b, i: (b, 0, 0, 0))
    o_spec = pl.BlockSpec((1, GROUP_WIDTH, tq), lambda b, i: (b, 0, i))
    state = [pltpu.VMEM((n_maps, 1, tq), F32), pltpu.VMEM((n_maps, V_ROWS, tq), F32),
             pltpu.VMEM((n_maps, KB, tq), F32), pltpu.VMEM((n_maps, 1, tq), F32)]
    return qa_spec, ka_spec, va_spec, o_spec, state


def _fx_kernel(qa_ref, ka_ref, va_ref, causal_ref, o_ref, *state, past, tq):
    q0 = past + pl.program_id(1) * tq
    acc_scr = _flash_tile(N_HEADS, lambda mp: mp, qa_ref, ka_ref, va_ref, state, q0,
                          lambda mp: causal_ref[...], None)
    for h in range(N_HEADS):
        o_ref[0, h * HEAD_DIM:(h + 1) * HEAD_DIM, :] = _normalized(acc_scr, h)


def _fx_attn(qa, ka, va, causal, past, tq):
    b, _, _, t = qa.shape
    qa_spec, ka_spec, va_spec, o_spec, scratch = _flash_specs(N_HEADS, tq, ka.shape[2])
    return pl.pallas_call(
        functools.partial(_fx_kernel, past=past, tq=tq),
        grid=(b, t // tq),
        in_specs=[qa_spec, ka_spec, va_spec, _full(causal.shape)],
        out_specs=o_spec,
        out_shape=jax.ShapeDtypeStruct((b, GROUP_WIDTH, t), F32),
        scratch_shapes=scratch,
        compiler_params=_cparams(("parallel", "parallel")),
        name="forgetting",
    )(qa, ka, va, causal)


def _df_kernel(qa_ref, ka_ref, va_ref, near_ref, lam_ref, g_ref, o_ref, *state, past, tq):
    q0 = past + pl.program_id(1) * tq
    acc_scr = _flash_tile(2 * N_HEADS, lambda mp: mp // 2, qa_ref, ka_ref, va_ref, state, q0,
                          lambda mp: near_ref[mp // 2], None)
    lam_init = lam_ref[4:5, 0:1]
    lam = (jnp.exp(jnp.sum(lam_ref[0:1, :] * lam_ref[1:2, :], axis=-1, keepdims=True))
           - jnp.exp(jnp.sum(lam_ref[2:3, :] * lam_ref[3:4, :], axis=-1, keepdims=True)) + lam_init)
    for h in range(N_HEADS):
        o = _normalized(acc_scr, 2 * h) - lam * _normalized(acc_scr, 2 * h + 1)
        o = o * lax.rsqrt(jnp.mean(o * o, axis=0, keepdims=True) + SUBLN_EPS) * g_ref[...]
        o_ref[0, h * HEAD_DIM:(h + 1) * HEAD_DIM, :] = o * (1.0 - lam_init)


def _df_attn(qa, ka, va, near, lam_rows, g_cols, past, tq):
    b, _, _, t = qa.shape
    qa_spec, ka_spec, va_spec, o_spec, scratch = _flash_specs(2 * N_HEADS, tq, ka.shape[2])
    return pl.pallas_call(
        functools.partial(_df_kernel, past=past, tq=tq),
        grid=(b, t // tq),
        in_specs=[qa_spec, ka_spec, va_spec, _full(near.shape), _full(lam_rows.shape), _full(g_cols.shape)],
        out_specs=o_spec,
        out_shape=jax.ShapeDtypeStruct((b, GROUP_WIDTH, t), F32),
        scratch_shapes=scratch,
        compiler_params=_cparams(("parallel", "parallel")),
        name="differential",
    )(qa, ka, va, near, lam_rows, g_cols)


def _ds_kernel(qix_ref, wix_ref, kix_ref, qa_ref, ka_ref, va_ref, near_ref, upto_ref, o_ref,
               key_scr, k16_scr, msk_scr, *state, past, tq, l_keys, topk):
    q0 = past + pl.program_id(1) * tq
    near0 = pl.multiple_of(q0, FRONT)
    n_far = _n_far_blocks(q0)
    wix = wix_ref[0] * (N_IDX_HEADS ** -0.5)

    def score_block(start, near):
        kix = kix_ref[0, pl.ds(start, KB), :]
        sc = jnp.zeros((KB, tq), F32)
        for hh in range(N_IDX_HEADS):
            sc = sc + wix[hh:hh + 1, :] * jnp.maximum(_dot(kix, qix_ref[0, hh]), 0.0)
        kidx = start + lax.broadcasted_iota(I32, (KB, tq), 0)
        ok = (kidx >= FRONT) & (kidx < FRONT + l_keys)
        if near:
            ok = ok & (near_ref[0] > 0.5 * NEG)
        bits = lax.bitcast_convert_type(jnp.where(ok, sc, -jnp.inf), I32)
        key = bits ^ ((bits >> 31) & 0x7FFFFFFF)
        key_scr[pl.ds(start, KB), :] = key
        k16_scr[pl.ds(start, KB), :] = (key >> HALF_BITS).astype(I16)

    def each_block(fn):
        fn(near0)

        def far(j, c):
            fn(_far_start(q0, j))
            return c

        lax.fori_loop(0, n_far, far, 0)

    score_block(near0, True)

    def score_far(j, c):
        score_block(_far_start(q0, j), False)
        return c

    lax.fori_loop(0, n_far, score_far, 0)

    def count16(cand):
        cand16 = cand.astype(I16)

        def blk(start):
            ones = jnp.where(k16_scr[pl.ds(start, KB), :] >= cand16, jnp.ones((), BF16), jnp.zeros((), BF16))
            parts = [ones[r:r + PACKED_ROWS] for r in range(0, KB, PACKED_ROWS)]
            while len(parts) > 1:
                parts = [parts[r] + parts[r + 1] for r in range(0, len(parts), 2)]
            return parts[0].astype(F32)

        acc = lax.fori_loop(0, n_far, lambda j, c: c + blk(_far_start(q0, j)), blk(near0))
        return jnp.sum(acc, axis=0, keepdims=True)

    def greedy16():
        def bit_body(i, u):
            cand = u | lax.shift_left(jnp.int32(1), HALF_BITS - 1 - i)
            return jnp.where(count16(cand - HALF_BIAS) >= topk, cand, u)
        return lax.fori_loop(0, HALF_BITS, bit_body, jnp.zeros((1, tq), I32))

    thr_hi = greedy16() - HALF_BIAS

    def refine_block(start):
        key = key_scr[pl.ds(start, KB), :]
        hi = key >> HALF_BITS
        low = jnp.where(hi > thr_hi, HALF_BIAS - 1,
                        jnp.where(hi < thr_hi, -HALF_BIAS, (key & (2 * HALF_BIAS - 1)) - HALF_BIAS))
        k16_scr[pl.ds(start, KB), :] = low.astype(I16)

    each_block(refine_block)
    thr = thr_hi * (2 * HALF_BIAS) + greedy16()

    def count2(start):
        key = key_scr[pl.ds(start, KB), :]
        return (jnp.sum((key >= thr).astype(I32), axis=0, keepdims=True),
                jnp.sum((key > thr).astype(I32), axis=0, keepdims=True))

    def count2_far(j, c):
        ge, gt = count2(_far_start(q0, j))
        return c[0] + ge, c[1] + gt

    n_ge, n_gt = lax.fori_loop(0, n_far, count2_far, count2(near0))
    tied = jnp.max(n_ge) > topk

    @pl.when(jnp.logical_not(tied))
    def _():
        def plain_block(start):
            key = key_scr[pl.ds(start, KB), :]
            msk_scr[pl.ds(start, KB), :] = jnp.where((key >= thr) & (key > KEY_NEG_INF), 0.0, NEG)
        each_block(plain_block)

    @pl.when(tied)
    def _():
        need = (topk - n_gt).astype(F32)

        def mask_block(start, seen):
            key = key_scr[pl.ds(start, KB), :]
            tie = key == thr
            rank = _dot(upto_ref[...], tie.astype(BF16)) + seen
            sel = (key > KEY_NEG_INF) & ((key > thr) | (tie & (rank <= need)))
            msk_scr[pl.ds(start, KB), :] = jnp.where(sel, 0.0, NEG)
            return seen + jnp.sum(tie.astype(F32), axis=0, keepdims=True)

        seen = lax.fori_loop(0, n_far, lambda i, c: mask_block(_far_start(q0, n_far - 1 - i), c),
                             jnp.zeros((1, tq), F32))
        mask_block(near0, seen)

    acc_scr = _flash_tile(N_HEADS, lambda mp: mp, qa_ref, ka_ref, va_ref, state, q0,
                          lambda mp: near_ref[mp] + msk_scr[pl.ds(near0, KB), :],
                          lambda start: msk_scr[pl.ds(start, KB), :])
    for h in range(N_HEADS):
        o_ref[0, h * HEAD_DIM:(h + 1) * HEAD_DIM, :] = _normalized(acc_scr, h)


def _ds_attn(qix, wix, kix, qa, ka, va, near, upto, past, tq, l_keys, topk):
    b, _, _, t = qa.shape
    lp = ka.shape[2]
    qa_spec, ka_spec, va_spec, o_spec, scratch = _flash_specs(N_HEADS, tq, lp)
    return pl.pallas_call(
        functools.partial(_ds_kernel, past=past, tq=tq, l_keys=l_keys, topk=topk),
        grid=(b, t // tq),
        in_specs=[pl.BlockSpec((1, N_IDX_HEADS, IDX_DIM, tq), lambda b, i: (b, 0, 0, i)),
                  pl.BlockSpec((1, N_IDX_HEADS, tq), lambda b, i: (b, 0, i)),
                  _resident((1, lp, IDX_DIM), lambda b, i: (b, 0, 0)),
                  qa_spec, ka_spec, va_spec, _full(near.shape), _full(upto.shape)],
        out_specs=o_spec,
        out_shape=jax.ShapeDtypeStruct((b, GROUP_WIDTH, t), F32),
        scratch_shapes=[pltpu.VMEM((lp, tq), I32), pltpu.VMEM((lp, tq), I16), pltpu.VMEM((lp, tq), F32)] + scratch,
        compiler_params=_cparams(("parallel", "parallel")),
        name="indexed_sparse",
    )(qix, wix, kix, qa, ka, va, near, upto)


def _post_kernel(x_ref, o0_ref, o1_ref, o2_ref, o3_ref, wo_ref, g_ref, wg_ref, wu_ref, wd_ref,
                 cw_ref, cb_ref, pc_ref, xo_ref, st_ref, uext, *, tm):
    t = pl.program_id(1)

    @pl.when(t == 0)
    def _():
        uext[0:SUBLANES, :] = pc_ref[0]

    @pl.when(t > 0)
    def _():
        uext[0:SUBLANES, :] = uext[tm:tm + SUBLANES, :]

    x = x_ref[0]
    for gi, ot_ref in enumerate((o0_ref, o1_ref, o2_ref, o3_ref)):
        x = x + lax.dot_general(ot_ref[0].astype(BF16), wo_ref[gi * GROUP_WIDTH:(gi + 1) * GROUP_WIDTH, :],
                                (((0,), (0,)), ((), ())), preferred_element_type=F32)
    hn = (x * lax.rsqrt(jnp.mean(x * x, axis=-1, keepdims=True) + RMS_EPS) * g_ref[...]).astype(BF16)
    u = _dot(hn, wg_ref[...])
    uext[SUBLANES:SUBLANES + tm, :] = u
    conv = cb_ref[...]
    conv = conv + cw_ref[0:1, :] * uext[SUBLANES - 2:SUBLANES - 2 + tm, :]
    conv = conv + cw_ref[1:2, :] * uext[SUBLANES - 1:SUBLANES - 1 + tm, :]
    conv = conv + cw_ref[2:3, :] * u
    hidden = conv * (1.0 / (1.0 + jnp.exp(-conv))) * _dot(hn, wu_ref[...])
    xo_ref[0] = x + _dot(hidden.astype(BF16), wd_ref[...])
    st_ref[0] = uext[tm:tm + SUBLANES, :]


def _post(x, outs, wo, g, wg, wu, wd, cw, cb, pconv, tm):
    b, t, d = x.shape
    f = wg.shape[1]
    row = lambda w: pl.BlockSpec((1, tm, w), lambda bi, ti: (bi, ti, 0))
    const = lambda shape: _resident(shape, lambda bi, ti: (0,) * len(shape))
    return pl.pallas_call(
        functools.partial(_post_kernel, tm=tm),
        grid=(b, t // tm),
        in_specs=[row(d)] + [pl.BlockSpec((1, GROUP_WIDTH, tm), lambda bi, ti: (bi, 0, ti))] * 4 + [
            const(wo.shape), const(g.shape), const(wg.shape), const(wu.shape), const(wd.shape),
            const(cw.shape), const(cb.shape),
            pl.BlockSpec((1, SUBLANES, f), lambda bi, ti: (bi, 0, 0))],
        out_specs=[row(d), pl.BlockSpec((1, SUBLANES, f), lambda bi, ti: (bi, 0, 0))],
        out_shape=[jax.ShapeDtypeStruct((b, t, d), F32),
                   jax.ShapeDtypeStruct((b, SUBLANES, f), F32)],
        scratch_shapes=[pltpu.VMEM((tm + SUBLANES, f), F32)],
        compiler_params=_cparams(("parallel", "arbitrary")),
        name="out_proj_mlp",
    )(x, *outs, wo, g, wg, wu, wd, cw, cb, pconv)


def _final_norm_kernel(x_ref, g_ref, o_ref):
    x = x_ref[...]
    o_ref[...] = x * lax.rsqrt(jnp.mean(x * x, axis=-1, keepdims=True) + RMS_EPS) * g_ref[...]


def _final_norm(x, g, tm):
    b, t, d = x.shape
    x2 = x.reshape(b * t, d)
    y = pl.pallas_call(
        _final_norm_kernel,
        grid=(b * t // tm,),
        in_specs=[pl.BlockSpec((tm, d), lambda i: (i, 0)), pl.BlockSpec((1, d), lambda i: (0, 0))],
        out_specs=pl.BlockSpec((tm, d), lambda i: (i, 0)),
        out_shape=jax.ShapeDtypeStruct(x2.shape, F32),
        compiler_params=_cparams(("parallel",)),
        name="final_norm",
    )(x2, g.reshape(1, d))
    return y.reshape(b, t, d)


def _t5_bucket(rel):
    half = T5_BUCKETS // 2
    max_exact = half // 2
    base = jnp.where(rel > 0, half, 0)
    n = jnp.abs(rel)
    nf = jnp.maximum(n, 1).astype(F32)
    large = max_exact + (jnp.log(nf / max_exact) / math.log(T5_MAX_DIST / max_exact)
                         * (half - max_exact)).astype(I32)
    large = jnp.minimum(large, half - 1)
    return base + jnp.where(n < max_exact, n, large)


def _bf16_part(x):
    bits = lax.bitcast_convert_type(x, jnp.uint32) & jnp.uint32(0xFFFF0000)
    return lax.bitcast_convert_type(bits, F32)


def _split3(x):
    hi = _bf16_part(x)
    mid = _bf16_part(x - hi)
    lo = _bf16_part(x - hi - mid)
    return jnp.stack([hi, mid, lo], axis=-1)


def _near_tables(t5_table, tq):
    kk = jnp.arange(KB, dtype=I32)[:, None] - FRONT
    qq = jnp.arange(tq, dtype=I32)[None, :]
    tab = t5_table.astype(F32).T
    far = tab[:, _t5_bucket(jnp.int32(-T5_MAX_DIST))]
    one_hot = (_t5_bucket(kk - qq)[:, :, None] == jnp.arange(T5_BUCKETS, dtype=I32)).astype(F32)
    bias = jnp.einsum("kqb,hb->hkq", one_hot, tab, precision=lax.Precision.HIGHEST)
    chunk_ok = (kk // CHUNK) <= (qq // CHUNK)
    near = jnp.where(chunk_ok[None], (bias - far[:, None, None]) * LOG2E, NEG)
    causal = jnp.where(kk <= qq, 0.0, NEG).astype(F32)
    return near, far * LOG2E, causal


def _pack_w_in(w_in):
    d = w_in.shape[0]
    fx0 = COL_SLAB + IDX_DIM + N_IDX_HEADS
    pad = jnp.zeros((d, LANES - SLAB_GATE - N_HEADS), w_in.dtype)
    return jnp.concatenate([w_in[:, :fx0], w_in[:, fx0 + 3 * GROUP_WIDTH:], pad,
                            w_in[:, fx0:fx0 + 3 * GROUP_WIDTH]], axis=1).astype(BF16)


def _with_past(past, new, l_pad):
    b, t = new.shape[:2]
    new = new.reshape(b, t, -1)
    if past is not None:
        new = jnp.concatenate([past.reshape(b, past.shape[1], -1).astype(F32), new], axis=1)
    return jnp.pad(new, ((0, 0), (0, l_pad - new.shape[1]), (0, 0)))


def _aug_queries(q, rows):
    b, t, m, _ = q.shape
    extra = jnp.zeros((m, LANES - HEAD_DIM), F32).at[:, :N_EXTRA].set(rows).at[:, N_EXTRA].set(1.0)
    extra = jnp.broadcast_to(extra[None, :, :, None], (b, m, LANES - HEAD_DIM, t))
    return jnp.concatenate([jnp.transpose(q, (0, 2, 3, 1)), extra], axis=2).astype(BF16)


def _aug_keys(k, cols, lp):
    b, l, h, _ = k.shape
    body = jnp.concatenate([k, cols, jnp.zeros((b, l, h, LANES - COL_FLAG), F32)], axis=-1)
    pad_row = jnp.zeros((LANES,), F32).at[COL_FLAG].set(NEG)
    pad = lambda n: jnp.broadcast_to(pad_row, (b, n, h, LANES))
    full = jnp.concatenate([pad(FRONT), body, pad(lp - FRONT - l)], axis=1)
    return jnp.transpose(full, (0, 2, 1, 3)).astype(BF16)


def _aug_values(v, lp):
    b, l, h, _ = v.shape
    vt = jnp.transpose(v, (0, 2, 3, 1))
    body = jnp.concatenate([vt, jnp.ones((b, h, 1, l), F32),
                            jnp.zeros((b, h, V_ROWS - HEAD_DIM - 1, l), F32)], axis=2)
    return jnp.pad(body, ((0, 0), (0, 0), (0, 0), (FRONT, lp - FRONT - l))).astype(BF16)


def _layer(x, past, lw, lam_init, t5_table, tm_proj, tm_post):
    (g_mix, w_in_p, b_forget, w_out, lam_q1, lam_k1, lam_q2, lam_k2, diff_subln,
     g_ffn, w_gate, w_up, w_down, conv_w, conv_b) = lw
    b, t, d = x.shape
    p_len = 0 if past is None else past[0].shape[1]
    l_keys = p_len + t
    l_pad = -(-l_keys // LANES) * LANES
    topk = min(TOPK_MAX, l_keys // 4)
    tq = min(t, TQ_MAX)
    assert p_len % FRONT == 0 and (t % TQ_MAX == 0 or t <= LANES) and FRONT == SB_KB == TQ_MAX
    lp = -(-max(FRONT + l_keys, p_len + t - tq + KB) // LANES) * LANES
    l_sb = p_len + t - tq + SB_KB
    (pk_sb, pv_sb, pk_df, pv_df, pk_ds, pv_ds, pk_ix, pk_fx, pv_fx, plogf, pconv) = (
        past if past is not None else (None,) * 11)

    bf = jnp.zeros((1, LANES), F32).at[0, SLAB_GATE:SLAB_GATE + N_HEADS].set(b_forget)
    y, lf = _proj(x.reshape(b * t, d), g_mix.reshape(1, d), w_in_p, bf, tm_proj)
    y = y.reshape(b, t, PROJ_PAD)
    grp = lambda i: y[:, :, i * GROUP_WIDTH:(i + 1) * GROUP_WIDTH]
    q_sb, k_sb, v_sb, q_df, k_df, v_df, q_ds, k_ds, v_ds = [grp(i) for i in range(9)]
    q_ix = y[:, :, COL_QIX:COL_SLAB]
    slab = y[:, :, COL_SLAB:COL_FX]
    k_ix = slab[:, :, :IDX_DIM]
    w_ix = slab[:, :, SLAB_WIX:SLAB_WIX + N_IDX_HEADS]
    q_fx, k_fx, v_fx = [y[:, :, COL_FX + i * GROUP_WIDTH:COL_FX + (i + 1) * GROUP_WIDTH] for i in range(3)]
    logf = lf.reshape(b, t, LANES)[:, :, SLAB_GATE:SLAB_GATE + N_HEADS]

    scale = HEAD_DIM ** -0.5
    hd = lambda a: a.reshape(b, -1, N_HEADS, HEAD_DIM)
    all_keys = lambda pst, new: hd(_with_past(pst, new, l_keys))
    near, far, causal = _near_tables(t5_table, tq)
    ones_cols = jnp.ones((b, l_keys, N_HEADS, N_EXTRA), F32)

    k_sb_all = jnp.transpose(hd(_with_past(pk_sb, k_sb, l_sb)), (0, 2, 1, 3)).astype(BF16)
    vt_sb_all = jnp.transpose(hd(_with_past(pv_sb, v_sb, l_sb)), (0, 2, 3, 1)).astype(BF16)
    later = (jnp.arange(SB_KB)[None, :] > jnp.arange(SB_KB)[:, None]).astype(BF16)
    o_sb = _sb_attn(jnp.transpose(hd(q_sb * scale), (0, 2, 3, 1)).astype(BF16), k_sb_all, vt_sb_all, later,
                    p_len, tq)

    half = jnp.arange(HEAD_DIM) < DIFF_DH
    q_df2 = jnp.where(jnp.stack([half, ~half])[None, None, None],
                      q_df.reshape(b, t, N_HEADS, 1, HEAD_DIM) * (DIFF_DH ** -0.5 * LOG2E), 0.0)
    lam_rows = jnp.zeros((SUBLANES, LANES), F32)
    for r, vec in enumerate((lam_q1, lam_k1, lam_q2, lam_k2)):
        lam_rows = lam_rows.at[r, :DIFF_DH].set(vec.astype(F32))
    lam_rows = lam_rows.at[4, :].set(lam_init)
    g_cols = jnp.broadcast_to(diff_subln.astype(F32)[:, None], (HEAD_DIM, tq))
    o_df = _df_attn(_aug_queries(q_df2.reshape(b, t, 2 * N_HEADS, HEAD_DIM),
                                 jnp.repeat(_split3(far[:N_HEADS]), 2, axis=0)),
                    _aug_keys(all_keys(pk_df, k_df), ones_cols, lp), _aug_values(all_keys(pv_df, v_df), lp),
                    near[:N_HEADS], lam_rows, g_cols, p_len, tq)

    kix_all = jnp.pad(_with_past(pk_ix, k_ix, l_keys), ((0, 0), (FRONT, lp - FRONT - l_keys), (0, 0))).astype(BF16)
    qix_t = jnp.transpose(q_ix.reshape(b, t, N_IDX_HEADS, IDX_DIM) * (IDX_DIM ** -0.5), (0, 2, 3, 1)).astype(BF16)
    upto = (jnp.arange(KB)[None, :] <= jnp.arange(KB)[:, None]).astype(BF16)
    o_ds = _ds_attn(qix_t, jnp.transpose(w_ix, (0, 2, 1)), kix_all,
                    _aug_queries(hd(q_ds * (scale * LOG2E)), _split3(far[N_HEADS:])),
                    _aug_keys(all_keys(pk_ds, k_ds), ones_cols, lp), _aug_values(all_keys(pv_ds, v_ds), lp),
                    near[N_HEADS:], upto, p_len, tq, l_keys, topk)

    c_all = _cumsum_seq(_with_past(plogf, logf, l_pad))
    c_keys = jnp.transpose(c_all[:, :, :l_keys], (0, 2, 1))
    o_fx = _fx_attn(_aug_queries(hd(q_fx * (scale * LOG2E)), jnp.ones((N_HEADS, N_EXTRA), F32)),
                    _aug_keys(all_keys(pk_fx, k_fx), _split3(-LOG2E * c_keys), lp),
                    _aug_values(all_keys(pv_fx, v_fx), lp), causal, p_len, tq)

    f = w_gate.shape[1]
    if pconv is None:
        pc = jnp.zeros((b, SUBLANES, f), F32)
    else:
        pc = jnp.pad(pconv.astype(F32), ((0, 0), (SUBLANES - (CONV_W - 1), 0), (0, 0)))
    cw = jnp.pad(conv_w.astype(F32), ((0, SUBLANES - CONV_W), (0, 0)))
    x_new, st = _post(x, (o_sb, o_df, o_ds, o_fx), w_out, g_ffn.reshape(1, d), w_gate, w_up,
                      w_down, cw, conv_b.reshape(1, f).astype(F32), pc, tm_post)

    rows = (hd(k_sb), hd(v_sb), k_df.reshape(b, t, N_HEADS, 2, DIFF_DH), hd(v_df), hd(k_ds), hd(v_ds),
            k_ix, hd(k_fx), hd(v_fx), logf, st[:, SUBLANES - (CONV_W - 1):])
    return x_new, rows


def _row_tile(n, cap):
    tm = min(n, cap)
    assert n % tm == 0
    return tm


def kernel(x_prompt, x_sample, cache_sb_k, cache_sb_v, cache_diff_k, cache_diff_v, cache_dsa_k, cache_dsa_v, cache_dsa_kidx, cache_fox_k, cache_fox_v, cache_fox_logf, state_ffn_conv, g_mix, w_in, b_forget, w_out, lam_q1, lam_k1, lam_q2, lam_k2, diff_subln, g_ffn, w_gate, w_up, w_down, conv_w, conv_b, t5_table, g_final):
    depth = w_in.shape[0]
    bp, tp, _ = x_prompt.shape
    bs, ts, _ = x_sample.shape
    xp, xs = x_prompt, x_sample
    p_rows, s_rows = [], []
    for l in range(depth):
        lw = (g_mix[l], _pack_w_in(w_in[l]), b_forget[l], w_out[l].astype(BF16), lam_q1[l], lam_k1[l],
              lam_q2[l], lam_k2[l], diff_subln[l], g_ffn[l], w_gate[l].astype(BF16), w_up[l].astype(BF16),
              w_down[l].astype(BF16), conv_w[l], conv_b[l])
        lam_init = 0.8 - 0.6 * math.exp(-0.3 * l)
        xp, rp = _layer(xp, None, lw, lam_init, t5_table, _row_tile(bp * tp, 512), _row_tile(tp, 256))
        s_past = (cache_sb_k[l], cache_sb_v[l], cache_diff_k[l], cache_diff_v[l], cache_dsa_k[l],
                  cache_dsa_v[l], cache_dsa_kidx[l], cache_fox_k[l], cache_fox_v[l], cache_fox_logf[l],
                  state_ffn_conv[l])
        xs, rs = _layer(xs, s_past, lw, lam_init, t5_table, _row_tile(bs * ts, 512), _row_tile(ts, 256))
        p_rows.append(rp)
        s_rows.append(rs)
    y_prompt = _final_norm(xp, g_final, _row_tile(bp * tp, 512))
    y_sample = _final_norm(xs, g_final, _row_tile(bs * ts, 512))
    stack = lambda rows, i: jnp.stack([r[i] for r in rows], axis=0)
    return ((y_prompt, y_sample) + tuple(stack(p_rows, i) for i in range(11))
            + tuple(stack(s_rows, i) for i in range(11)))
```

```python
import functools
import math

import jax
import jax.numpy as jnp
from jax import lax
from jax.experimental import pallas as pl
from jax.experimental.pallas import tpu as pltpu

F32 = jnp.float32
BF16 = jnp.bfloat16
I32 = jnp.int32
I16 = jnp.int16

N_HEADS = 4
HEAD_DIM = 64
GROUP_WIDTH = N_HEADS * HEAD_DIM
DIFF_DH = HEAD_DIM // 2
N_IDX_HEADS = 8
IDX_DIM = 64
CHUNK = 64
TOPK_MAX = 256
T5_BUCKETS = 32
T5_MAX_DIST = 128
CONV_W = 3
RMS_EPS = 1e-6
SUBLN_EPS = 1e-5

LANES = 128
SUBLANES = 8
SB_KB = 256
KB = 512
FRONT = 256
TQ_MAX = 256
N_EXTRA = 3
COL_FLAG = HEAD_DIM + N_EXTRA
V_ROWS = 80
NEG = -1e30
SB_DEAD = -110.0
HALF_BITS = 16
HALF_BIAS = 2 ** (HALF_BITS - 1)
PACKED_ROWS = 16
KEY_NEG_INF = -2139095041
LOG2E = math.log2(math.e)
VMEM_LIMIT = 56 * 1024 * 1024

COL_QIX = 9 * GROUP_WIDTH
COL_SLAB = COL_QIX + N_IDX_HEADS * IDX_DIM
SLAB_WIX = IDX_DIM
SLAB_GATE = IDX_DIM + N_IDX_HEADS
COL_FX = COL_SLAB + LANES
PROJ_PAD = COL_FX + 3 * GROUP_WIDTH


def _cparams(sem):
    return pltpu.CompilerParams(dimension_semantics=sem, vmem_limit_bytes=VMEM_LIMIT)


def _resident(shape, index_map):
    return pl.BlockSpec(shape, index_map, pipeline_mode=pl.Buffered(1))


def _full(shape):
    return _resident(shape, lambda b, i: (0,) * len(shape))


def _log_sigmoid(z):
    return jnp.minimum(z, 0.0) - jnp.log(1.0 + jnp.exp(-jnp.abs(z)))


def _dot(a, b):
    return jnp.dot(a, b, preferred_element_type=F32)


def _proj_kernel(x_ref, g_ref, w_ref, bf_ref, y_ref, lf_ref):
    x = x_ref[...]
    hn = x * lax.rsqrt(jnp.mean(x * x, axis=-1, keepdims=True) + RMS_EPS) * g_ref[...]
    y = _dot(hn.astype(BF16), w_ref[...])
    y_ref[...] = y
    lf_ref[...] = _log_sigmoid(y[:, COL_SLAB:COL_SLAB + LANES] + bf_ref[...])


def _proj(x2, g, w, bf, tm):
    n, d = x2.shape
    return pl.pallas_call(
        _proj_kernel,
        grid=(n // tm,),
        in_specs=[pl.BlockSpec((tm, d), lambda i: (i, 0)),
                  pl.BlockSpec((1, d), lambda i: (0, 0)),
                  _resident((d, PROJ_PAD), lambda i: (0, 0)),
                  pl.BlockSpec((1, LANES), lambda i: (0, 0))],
        out_specs=[pl.BlockSpec((tm, PROJ_PAD), lambda i: (i, 0)),
                   pl.BlockSpec((tm, LANES), lambda i: (i, 0))],
        out_shape=[jax.ShapeDtypeStruct((n, PROJ_PAD), F32),
                   jax.ShapeDtypeStruct((n, LANES), F32)],
        compiler_params=_cparams(("parallel",)),
        name="proj",
    )(x2, g, w, bf)


def _cumsum_kernel(x_ref, o_ref):
    nch, rows, _ = x_ref.shape
    lane = lax.broadcasted_iota(I32, (rows, LANES), 1)

    def body(c, carry):
        x = x_ref[c]
        d = 1
        while d < LANES:
            x = x + jnp.where(lane >= d, pltpu.roll(x, d, 1), 0.0)
            d *= 2
        x = x + carry
        o_ref[c] = x
        return jnp.broadcast_to(x[:, LANES - 1:LANES], (rows, LANES))

    lax.fori_loop(0, nch, body, jnp.zeros((rows, LANES), F32))


def _cumsum_seq(logf_all):
    b, l, h = logf_all.shape
    rows = b * h
    rows_pad = -(-rows // SUBLANES) * SUBLANES
    x = jnp.transpose(logf_all, (0, 2, 1)).reshape(rows, l // LANES, LANES)
    x = jnp.pad(jnp.transpose(x, (1, 0, 2)), ((0, 0), (0, rows_pad - rows), (0, 0)))
    c = pl.pallas_call(
        _cumsum_kernel,
        out_shape=jax.ShapeDtypeStruct(x.shape, F32),
        compiler_params=pltpu.CompilerParams(vmem_limit_bytes=VMEM_LIMIT),
        name="cumsum",
    )(x)
    return jnp.transpose(c[:, :rows], (1, 0, 2)).reshape(b, h, l)


def _sb_kernel(qt_ref, k_ref, vt_ref, later_ref, o_ref, run_scr, acc_scr, *, past, tq):
    q0 = pl.multiple_of(past + pl.program_id(1) * tq, SB_KB)
    run_scr[...] = jnp.zeros(run_scr.shape, F32)
    acc_scr[...] = jnp.zeros(acc_scr.shape, F32)
    later = later_ref[...]

    def block(start, strict):
        logits = [_dot(k_ref[0, h, pl.ds(start, SB_KB), :], qt_ref[0, h]) for h in range(N_HEADS)]
        log_beta, log_keep, pieces = [], [], []
        for h in range(N_HEADS):
            z = logits[h]
            ls = _log_sigmoid(z)
            lk = ls - z
            if strict is not None:
                lk = jnp.where(strict, lk, 0.0)
            hi = lk.astype(BF16)
            rem = lk - hi.astype(F32)
            mid = rem.astype(BF16)
            lo = (rem - mid.astype(F32)).astype(BF16)
            log_beta.append(ls)
            log_keep.append(lk)
            pieces.append((hi, mid, lo))
        between = [_dot(later, pieces[h][0]) + _dot(later, pieces[h][1]) + _dot(later, pieces[h][2])
                   for h in range(N_HEADS)]
        weights = []
        for h in range(N_HEADS):
            a = jnp.exp(log_beta[h] + between[h] + run_scr[h])
            if strict is not None:
                a = jnp.where(strict, a, 0.0)
            weights.append(a.astype(BF16))
        for h in range(N_HEADS):
            acc_scr[h] = acc_scr[h] + _dot(vt_ref[0, h, :, pl.ds(start, SB_KB)], weights[h])
            run_scr[h] = run_scr[h] + jnp.sum(log_keep[h], axis=0, keepdims=True)

    block(q0, lax.broadcasted_iota(I32, (SB_KB, tq), 0) < lax.broadcasted_iota(I32, (SB_KB, tq), 1))

    def cond(carry):
        return (carry[0] >= 0) & (carry[1] > SB_DEAD)

    def body(carry):
        block(pl.multiple_of(carry[0], SB_KB), None)
        return carry[0] - SB_KB, jnp.max(run_scr[...])

    lax.while_loop(cond, body, (q0 - SB_KB, jnp.max(run_scr[...])))
    for h in range(N_HEADS):
        o_ref[0, h * HEAD_DIM:(h + 1) * HEAD_DIM, :] = acc_scr[h]


def _sb_attn(qt, k, vt, later, past, tq):
    b, _, _, t = qt.shape
    l_sb = k.shape[2]
    return pl.pallas_call(
        functools.partial(_sb_kernel, past=past, tq=tq),
        grid=(b, t // tq),
        in_specs=[pl.BlockSpec((1, N_HEADS, HEAD_DIM, tq), lambda b, i: (b, 0, 0, i)),
                  _resident((1, N_HEADS, l_sb, HEAD_DIM), lambda b, i: (b, 0, 0, 0)),
                  _resident((1, N_HEADS, HEAD_DIM, l_sb), lambda b, i: (b, 0, 0, 0)),
                  _full(later.shape)],
        out_specs=pl.BlockSpec((1, GROUP_WIDTH, tq), lambda b, i: (b, 0, i)),
        out_shape=jax.ShapeDtypeStruct((b, GROUP_WIDTH, t), F32),
        scratch_shapes=[pltpu.VMEM((N_HEADS, 1, tq), F32), pltpu.VMEM((N_HEADS, HEAD_DIM, tq), F32)],
        compiler_params=_cparams(("parallel", "parallel")),
        name="stick_breaking",
    )(qt, k, vt, later)


def _n_far_blocks(q0):
    return jnp.maximum(q0 - FRONT + KB - 1, 0) // KB


def _far_start(q0, j):
    return pl.multiple_of(q0 - KB * (j + 1), FRONT)


def _flash_tile(n_maps, head_of, qa_ref, ka_ref, va_ref, state, q0, near_add, far_add):
    m_scr, acc_scr, s_scr, smax_scr = state
    m_scr[...] = jnp.full(m_scr.shape, NEG, F32)
    acc_scr[...] = jnp.zeros(acc_scr.shape, F32)

    def start_of(i):
        return pl.multiple_of(q0 - KB * i, FRONT)

    def scores(mp, start, add):
        s = _dot(ka_ref[0, head_of(mp), pl.ds(start, KB), :], qa_ref[0, mp])
        if add is not None:
            s = s + add(mp)
        s_scr[mp] = s
        smax_scr[mp] = jnp.max(s, axis=0, keepdims=True)

    def absorb(mp, start):
        m_old = m_scr[mp]
        m_new = jnp.maximum(m_old, smax_scr[mp])
        p = jnp.exp2(s_scr[mp] - m_new).astype(BF16)
        acc_scr[mp] = (acc_scr[mp] * jnp.exp2(m_old - m_new)
                       + _dot(va_ref[0, head_of(mp), :, pl.ds(start, KB)], p))
        m_scr[mp] = m_new

    for mp in range(n_maps):
        scores(mp, start_of(0), near_add)

    def body(i, c):
        nxt = start_of(i + 1)
        for mp in range(n_maps):
            absorb(mp, start_of(i))
            scores(mp, nxt, None if far_add is None else (lambda mp_: far_add(nxt)))
        return c

    n_far = _n_far_blocks(q0)
    lax.fori_loop(0, n_far, body, 0)
    for mp in range(n_maps):
        absorb(mp, start_of(n_far))
    return acc_scr


def _normalized(acc_scr, mp):
    return acc_scr[mp, 0:HEAD_DIM, :] / acc_scr[mp, HEAD_DIM:HEAD_DIM + 1, :]


def _flash_specs(n_maps, tq, lp):
    qa_spec = pl.BlockSpec((1, n_maps, LANES, tq), lambda b, i: (b, 0, 0, i))
    ka_spec = _resident((1, N_HEADS, lp, LANES), lambda b, i: (b, 0, 0, 0))
    va_spec = _resident((1, N_HEADS, V_ROWS, lp), lambda b, i: (b, 0, 0, 0))
    o_spec = pl.BlockSpec((1, GROUP_WIDTH, tq), lambda b, i: (b, 0, i))
    state = [pltpu.VMEM((n_maps, 1, tq), F32), pltpu.VMEM((n_maps, V_ROWS, tq), F32),
             pltpu.VMEM((n_maps, KB, tq), F32), pltpu.VMEM((n_maps, 1, tq), F32)]
    return qa_spec, ka_spec, va_spec, o_spec, state


def _fx_kernel(qa_ref, ka_ref, va_ref, causal_ref, o_ref, *state, past, tq):
    q0 = past + pl.program_id(1) * tq
    acc_scr = _flash_tile(N_HEADS, lambda mp: mp, qa_ref, ka_ref, va_ref, state, q0,
                          lambda mp: causal_ref[...], None)
    for h in range(N_HEADS):
        o_ref[0, h * HEAD_DIM:(h + 1) * HEAD_DIM, :] = _normalized(acc_scr, h)


def _fx_attn(qa, ka, va, causal, past, tq):
    b, _, _, t = qa.shape
    qa_spec, ka_spec, va_spec, o_spec, scratch = _flash_specs(N_HEADS, tq, ka.shape[2])
    return pl.pallas_call(
        functools.partial(_fx_kernel, past=past, tq=tq),
        grid=(b, t // tq),
        in_specs=[qa_spec, ka_spec, va_spec, _full(causal.shape)],
        out_specs=o_spec,
        out_shape=jax.ShapeDtypeStruct((b, GROUP_WIDTH, t), F32),
        scratch_shapes=scratch,
        compiler_params=_cparams(("parallel", "parallel")),
        name="forgetting",
    )(qa, ka, va, causal)


def _df_kernel(qa_ref, ka_ref, va_ref, near_ref, lam_ref, g_ref, o_ref, *state, past, tq):
    q0 = past + pl.program_id(1) * tq
    acc_scr = _flash_tile(2 * N_HEADS, lambda mp: mp // 2, qa_ref, ka_ref, va_ref, state, q0,
                          lambda mp: near_ref[mp // 2], None)
    lam_init = lam_ref[4:5, 0:1]
    lam = (jnp.exp(jnp.sum(lam_ref[0:1, :] * lam_ref[1:2, :], axis=-1, keepdims=True))
           - jnp.exp(jnp.sum(lam_ref[2:3, :] * lam_ref[3:4, :], axis=-1, keepdims=True)) + lam_init)
    for h in range(N_HEADS):
        o = _normalized(acc_scr, 2 * h) - lam * _normalized(acc_scr, 2 * h + 1)
        o = o * lax.rsqrt(jnp.mean(o * o, axis=0, keepdims=True) + SUBLN_EPS) * g_ref[...]
        o_ref[0, h * HEAD_DIM:(h + 1) * HEAD_DIM, :] = o * (1.0 - lam_init)


def _df_attn(qa, ka, va, near, lam_rows, g_cols, past, tq):
    b, _, _, t = qa.shape
    qa_spec, ka_spec, va_spec, o_spec, scratch = _flash_specs(2 * N_HEADS, tq, ka.shape[2])
    return pl.pallas_call(
        functools.partial(_df_kernel, past=past, tq=tq),
        grid=(b, t // tq),
        in_specs=[qa_spec, ka_spec, va_spec, _full(near.shape), _full(lam_rows.shape), _full(g_cols.shape)],
        out_specs=o_spec,
        out_shape=jax.ShapeDtypeStruct((b, GROUP_WIDTH, t), F32),
        scratch_shapes=scratch,
        compiler_params=_cparams(("parallel", "parallel")),
        name="differential",
    )(qa, ka, va, near, lam_rows, g_cols)


def _ds_kernel(qix_ref, wix_ref, kix_ref, qa_ref, ka_ref, va_ref, near_ref, upto_ref, o_ref,
               key_scr, k16_scr, msk_scr, *state, past, tq, l_keys, topk):
    q0 = past + pl.program_id(1) * tq
    near0 = pl.multiple_of(q0, FRONT)
    n_far = _n_far_blocks(q0)
    wix = wix_ref[0] * (N_IDX_HEADS ** -0.5)

    def score_block(start, near):
        kix = kix_ref[0, pl.ds(start, KB), :]
        sc = jnp.zeros((KB, tq), F32)
        for hh in range(N_IDX_HEADS):
            sc = sc + wix[hh:hh + 1, :] * jnp.maximum(_dot(kix, qix_ref[0, hh]), 0.0)
        kidx = start + lax.broadcasted_iota(I32, (KB, tq), 0)
        ok = (kidx >= FRONT) & (kidx < FRONT + l_keys)
        if near:
            ok = ok & (near_ref[0] > 0.5 * NEG)
        bits = lax.bitcast_convert_type(jnp.where(ok, sc, -jnp.inf), I32)
        key = bits ^ ((bits >> 31) & 0x7FFFFFFF)
        key_scr[pl.ds(start, KB), :] = key
        k16_scr[pl.ds(start, KB), :] = (key >> HALF_BITS).astype(I16)

    def each_block(fn):
        fn(near0)

        def far(j, c):
            fn(_far_start(q0, j))
            return c

        lax.fori_loop(0, n_far, far, 0)

    score_block(near0, True)

    def score_far(j, c):
        score_block(_far_start(q0, j), False)
        return c

    lax.fori_loop(0, n_far, score_far, 0)

    def count16(cand):
        cand16 = cand.astype(I16)

        def blk(start):
            ones = jnp.where(k16_scr[pl.ds(start, KB), :] >= cand16, jnp.ones((), BF16), jnp.zeros((), BF16))
            parts = [ones[r:r + PACKED_ROWS] for r in range(0, KB, PACKED_ROWS)]
            while len(parts) > 1:
                parts = [parts[r] + parts[r + 1] for r in range(0, len(parts), 2)]
            return parts[0].astype(F32)

        acc = lax.fori_loop(0, n_far, lambda j, c: c + blk(_far_start(q0, j)), blk(near0))
        return jnp.sum(acc, axis=0, keepdims=True)

    def greedy16(n_all):
        def bit_body(i, carry):
            u, n_u = carry
            cand = u | lax.shift_left(jnp.int32(1), HALF_BITS - 1 - i)
            n_cand = count16(cand - HALF_BIAS)
            take = n_cand >= topk
            return jnp.where(take, cand, u), jnp.where(take, n_cand, n_u)
        return lax.fori_loop(0, HALF_BITS, bit_body, (jnp.zeros((1, tq), I32), n_all))

    n_scratch = ((n_far + 1) * KB).astype(F32)
    hi_u, n_hi = greedy16(jnp.full((1, tq), n_scratch, F32))
    thr_hi = hi_u - HALF_BIAS

    def refine_block(start):
        key = key_scr[pl.ds(start, KB), :]
        hi = key >> HALF_BITS
        low = jnp.where(hi > thr_hi, HALF_BIAS - 1,
                        jnp.where(hi < thr_hi, -HALF_BIAS, (key & (2 * HALF_BIAS - 1)) - HALF_BIAS))
        k16_scr[pl.ds(start, KB), :] = low.astype(I16)

    each_block(refine_block)
    lo_u, n_ge = greedy16(n_hi)
    thr = thr_hi * (2 * HALF_BIAS) + lo_u

    tied = jnp.max(n_ge) > topk

    @pl.when(jnp.logical_not(tied))
    def _():
        def plain_block(start):
            key = key_scr[pl.ds(start, KB), :]
            msk_scr[pl.ds(start, KB), :] = jnp.where((key >= thr) & (key > KEY_NEG_INF), 0.0, NEG)
        each_block(plain_block)

    @pl.when(tied)
    def _():
        def above(start):
            return jnp.sum((key_scr[pl.ds(start, KB), :] > thr).astype(I32), axis=0, keepdims=True)

        n_gt = lax.fori_loop(0, n_far, lambda j, c: c + above(_far_start(q0, j)), above(near0))
        need = (topk - n_gt).astype(F32)

        def mask_block(start, seen):
            key = key_scr[pl.ds(start, KB), :]
            tie = key == thr
            rank = _dot(upto_ref[...], tie.astype(BF16)) + seen
            sel = (key > KEY_NEG_INF) & ((key > thr) | (tie & (rank <= need)))
            msk_scr[pl.ds(start, KB), :] = jnp.where(sel, 0.0, NEG)
            return seen + jnp.sum(tie.astype(F32), axis=0, keepdims=True)

        seen = lax.fori_loop(0, n_far, lambda i, c: mask_block(_far_start(q0, n_far - 1 - i), c),
                             jnp.zeros((1, tq), F32))
        mask_block(near0, seen)

    acc_scr = _flash_tile(N_HEADS, lambda mp: mp, qa_ref, ka_ref, va_ref, state, q0,
                          lambda mp: near_ref[mp] + msk_scr[pl.ds(near0, KB), :],
                          lambda start: msk_scr[pl.ds(start, KB), :])
    for h in range(N_HEADS):
        o_ref[0, h * HEAD_DIM:(h + 1) * HEAD_DIM, :] = _normalized(acc_scr, h)


def _ds_attn(qix, wix, kix, qa, ka, va, near, upto, past, tq, l_keys, topk):
    b, _, _, t = qa.shape
    lp = ka.shape[2]
    qa_spec, ka_spec, va_spec, o_spec, scratch = _flash_specs(N_HEADS, tq, lp)
    return pl.pallas_call(
        functools.partial(_ds_kernel, past=past, tq=tq, l_keys=l_keys, topk=topk),
        grid=(b, t // tq),
        in_specs=[pl.BlockSpec((1, N_IDX_HEADS, IDX_DIM, tq), lambda b, i: (b, 0, 0, i)),
                  pl.BlockSpec((1, N_IDX_HEADS, tq), lambda b, i: (b, 0, i)),
                  _resident((1, lp, IDX_DIM), lambda b, i: (b, 0, 0)),
                  qa_spec, ka_spec, va_spec, _full(near.shape), _full(upto.shape)],
        out_specs=o_spec,
        out_shape=jax.ShapeDtypeStruct((b, GROUP_WIDTH, t), F32),
        scratch_shapes=[pltpu.VMEM((lp, tq), I32), pltpu.VMEM((lp, tq), I16), pltpu.VMEM((lp, tq), F32)] + scratch,
        compiler_params=_cparams(("parallel", "parallel")),
        name="indexed_sparse",
    )(qix, wix, kix, qa, ka, va, near, upto)


def _post_kernel(x_ref, o0_ref, o1_ref, o2_ref, o3_ref, wo_ref, g_ref, wg_ref, wu_ref, wd_ref,
                 cw_ref, cb_ref, pc_ref, xo_ref, st_ref, uext, *, tm):
    t = pl.program_id(1)

    @pl.when(t == 0)
    def _():
        uext[0:SUBLANES, :] = pc_ref[0]

    @pl.when(t > 0)
    def _():
        uext[0:SUBLANES, :] = uext[tm:tm + SUBLANES, :]

    x = x_ref[0]
    for gi, ot_ref in enumerate((o0_ref, o1_ref, o2_ref, o3_ref)):
        x = x + lax.dot_general(ot_ref[0].astype(BF16), wo_ref[gi * GROUP_WIDTH:(gi + 1) * GROUP_WIDTH, :],
                                (((0,), (0,)), ((), ())), preferred_element_type=F32)
    hn = (x * lax.rsqrt(jnp.mean(x * x, axis=-1, keepdims=True) + RMS_EPS) * g_ref[...]).astype(BF16)
    u = _dot(hn, wg_ref[...])
    uext[SUBLANES:SUBLANES + tm, :] = u
    conv = cb_ref[...]
    conv = conv + cw_ref[0:1, :] * uext[SUBLANES - 2:SUBLANES - 2 + tm, :]
    conv = conv + cw_ref[1:2, :] * uext[SUBLANES - 1:SUBLANES - 1 + tm, :]
    conv = conv + cw_ref[2:3, :] * u
    hidden = conv * (1.0 / (1.0 + jnp.exp(-conv))) * _dot(hn, wu_ref[...])
    xo_ref[0] = x + _dot(hidden.astype(BF16), wd_ref[...])
    st_ref[0] = uext[tm:tm + SUBLANES, :]


def _post(x, outs, wo, g, wg, wu, wd, cw, cb, pconv, tm):
    b, t, d = x.shape
    f = wg.shape[1]
    row = lambda w: pl.BlockSpec((1, tm, w), lambda bi, ti: (bi, ti, 0))
    const = lambda shape: _resident(shape, lambda bi, ti: (0,) * len(shape))
    return pl.pallas_call(
        functools.partial(_post_kernel, tm=tm),
        grid=(b, t // tm),
        in_specs=[row(d)] + [pl.BlockSpec((1, GROUP_WIDTH, tm), lambda bi, ti: (bi, 0, ti))] * 4 + [
            const(wo.shape), const(g.shape), const(wg.shape), const(wu.shape), const(wd.shape),
            const(cw.shape), const(cb.shape),
            pl.BlockSpec((1, SUBLANES, f), lambda bi, ti: (bi, 0, 0))],
        out_specs=[row(d), pl.BlockSpec((1, SUBLANES, f), lambda bi, ti: (bi, 0, 0))],
        out_shape=[jax.ShapeDtypeStruct((b, t, d), F32),
                   jax.ShapeDtypeStruct((b, SUBLANES, f), F32)],
        scratch_shapes=[pltpu.VMEM((tm + SUBLANES, f), F32)],
        compiler_params=_cparams(("parallel", "arbitrary")),
        name="out_proj_mlp",
    )(x, *outs, wo, g, wg, wu, wd, cw, cb, pconv)


def _final_norm_kernel(x_ref, g_ref, o_ref):
    x = x_ref[...]
    o_ref[...] = x * lax.rsqrt(jnp.mean(x * x, axis=-1, keepdims=True) + RMS_EPS) * g_ref[...]


def _final_norm(x, g, tm):
    b, t, d = x.shape
    x2 = x.reshape(b * t, d)
    y = pl.pallas_call(
        _final_norm_kernel,
        grid=(b * t // tm,),
        in_specs=[pl.BlockSpec((tm, d), lambda i: (i, 0)), pl.BlockSpec((1, d), lambda i: (0, 0))],
        out_specs=pl.BlockSpec((tm, d), lambda i: (i, 0)),
        out_shape=jax.ShapeDtypeStruct(x2.shape, F32),
        compiler_params=_cparams(("parallel",)),
        name="final_norm",
    )(x2, g.reshape(1, d))
    return y.reshape(b, t, d)


def _t5_bucket(rel):
    half = T5_BUCKETS // 2
    max_exact = half // 2
    base = jnp.where(rel > 0, half, 0)
    n = jnp.abs(rel)
    nf = jnp.maximum(n, 1).astype(F32)
    large = max_exact + (jnp.log(nf / max_exact) / math.log(T5_MAX_DIST / max_exact)
                         * (half - max_exact)).astype(I32)
    large = jnp.minimum(large, half - 1)
    return base + jnp.where(n < max_exact, n, large)


def _bf16_part(x):
    bits = lax.bitcast_convert_type(x, jnp.uint32) & jnp.uint32(0xFFFF0000)
    return lax.bitcast_convert_type(bits, F32)


def _split3(x):
    hi = _bf16_part(x)
    mid = _bf16_part(x - hi)
    lo = _bf16_part(x - hi - mid)
    return jnp.stack([hi, mid, lo], axis=-1)


def _near_tables(t5_table, tq):
    kk = jnp.arange(KB, dtype=I32)[:, None] - FRONT
    qq = jnp.arange(tq, dtype=I32)[None, :]
    tab = t5_table.astype(F32).T
    far = tab[:, _t5_bucket(jnp.int32(-T5_MAX_DIST))]
    one_hot = (_t5_bucket(kk - qq)[:, :, None] == jnp.arange(T5_BUCKETS, dtype=I32)).astype(F32)
    bias = jnp.einsum("kqb,hb->hkq", one_hot, tab, precision=lax.Precision.HIGHEST)
    chunk_ok = (kk // CHUNK) <= (qq // CHUNK)
    near = jnp.where(chunk_ok[None], (bias - far[:, None, None]) * LOG2E, NEG)
    causal = jnp.where(kk <= qq, 0.0, NEG).astype(F32)
    return near, far * LOG2E, causal


def _pack_w_in(w_in):
    d = w_in.shape[0]
    fx0 = COL_SLAB + IDX_DIM + N_IDX_HEADS
    pad = jnp.zeros((d, LANES - SLAB_GATE - N_HEADS), w_in.dtype)
    return jnp.concatenate([w_in[:, :fx0], w_in[:, fx0 + 3 * GROUP_WIDTH:], pad,
                            w_in[:, fx0:fx0 + 3 * GROUP_WIDTH]], axis=1).astype(BF16)


def _with_past(past, new, l_pad):
    b, t = new.shape[:2]
    new = new.reshape(b, t, -1)
    if past is not None:
        new = jnp.concatenate([past.reshape(b, past.shape[1], -1).astype(F32), new], axis=1)
    return jnp.pad(new, ((0, 0), (0, l_pad - new.shape[1]), (0, 0)))


def _aug_queries(q, rows):
    b, t, m, _ = q.shape
    extra = jnp.zeros((m, LANES - HEAD_DIM), F32).at[:, :N_EXTRA].set(rows).at[:, N_EXTRA].set(1.0)
    extra = jnp.broadcast_to(extra[None, :, :, None], (b, m, LANES - HEAD_DIM, t))
    return jnp.concatenate([jnp.transpose(q, (0, 2, 3, 1)), extra], axis=2).astype(BF16)


def _aug_keys(k, cols, lp):
    b, l, h, _ = k.shape
    body = jnp.concatenate([k, cols, jnp.zeros((b, l, h, LANES - COL_FLAG), F32)], axis=-1)
    pad_row = jnp.zeros((LANES,), F32).at[COL_FLAG].set(NEG)
    pad = lambda n: jnp.broadcast_to(pad_row, (b, n, h, LANES))
    full = jnp.concatenate([pad(FRONT), body, pad(lp - FRONT - l)], axis=1)
    return jnp.transpose(full, (0, 2, 1, 3)).astype(BF16)


def _aug_values(v, lp):
    b, l, h, _ = v.shape
    vt = jnp.transpose(v, (0, 2, 3, 1))
    body = jnp.concatenate([vt, jnp.ones((b, h, 1, l), F32),
                            jnp.zeros((b, h, V_ROWS - HEAD_DIM - 1, l), F32)], axis=2)
    return jnp.pad(body, ((0, 0), (0, 0), (0, 0), (FRONT, lp - FRONT - l))).astype(BF16)


def _layer(x, past, lw, lam_init, t5_table, tm_proj, tm_post):
    (g_mix, w_in_p, b_forget, w_out, lam_q1, lam_k1, lam_q2, lam_k2, diff_subln,
     g_ffn, w_gate, w_up, w_down, conv_w, conv_b) = lw
    b, t, d = x.shape
    p_len = 0 if past is None else past[0].shape[1]
    l_keys = p_len + t
    l_pad = -(-l_keys // LANES) * LANES
    topk = min(TOPK_MAX, l_keys // 4)
    tq = min(t, TQ_MAX)
    assert p_len % FRONT == 0 and (t % TQ_MAX == 0 or t <= LANES) and FRONT == SB_KB == TQ_MAX
    lp = -(-max(FRONT + l_keys, p_len + t - tq + KB) // LANES) * LANES
    l_sb = p_len + t - tq + SB_KB
    (pk_sb, pv_sb, pk_df, pv_df, pk_ds, pv_ds, pk_ix, pk_fx, pv_fx, plogf, pconv) = (
        past if past is not None else (None,) * 11)

    bf = jnp.zeros((1, LANES), F32).at[0, SLAB_GATE:SLAB_GATE + N_HEADS].set(b_forget)
    y, lf = _proj(x.reshape(b * t, d), g_mix.reshape(1, d), w_in_p, bf, tm_proj)
    y = y.reshape(b, t, PROJ_PAD)
    grp = lambda i: y[:, :, i * GROUP_WIDTH:(i + 1) * GROUP_WIDTH]
    q_sb, k_sb, v_sb, q_df, k_df, v_df, q_ds, k_ds, v_ds = [grp(i) for i in range(9)]
    q_ix = y[:, :, COL_QIX:COL_SLAB]
    slab = y[:, :, COL_SLAB:COL_FX]
    k_ix = slab[:, :, :IDX_DIM]
    w_ix = slab[:, :, SLAB_WIX:SLAB_WIX + N_IDX_HEADS]
    q_fx, k_fx, v_fx = [y[:, :, COL_FX + i * GROUP_WIDTH:COL_FX + (i + 1) * GROUP_WIDTH] for i in range(3)]
    logf = lf.reshape(b, t, LANES)[:, :, SLAB_GATE:SLAB_GATE + N_HEADS]

    scale = HEAD_DIM ** -0.5
    hd = lambda a: a.reshape(b, -1, N_HEADS, HEAD_DIM)
    all_keys = lambda pst, new: hd(_with_past(pst, new, l_keys))
    near, far, causal = _near_tables(t5_table, tq)
    ones_cols = jnp.ones((b, l_keys, N_HEADS, N_EXTRA), F32)

    k_sb_all = jnp.transpose(hd(_with_past(pk_sb, k_sb, l_sb)), (0, 2, 1, 3)).astype(BF16)
    vt_sb_all = jnp.transpose(hd(_with_past(pv_sb, v_sb, l_sb)), (0, 2, 3, 1)).astype(BF16)
    later = (jnp.arange(SB_KB)[None, :] > jnp.arange(SB_KB)[:, None]).astype(BF16)
    o_sb = _sb_attn(jnp.transpose(hd(q_sb * scale), (0, 2, 3, 1)).astype(BF16), k_sb_all, vt_sb_all, later,
                    p_len, tq)

    half = jnp.arange(HEAD_DIM) < DIFF_DH
    q_df2 = jnp.where(jnp.stack([half, ~half])[None, None, None],
                      q_df.reshape(b, t, N_HEADS, 1, HEAD_DIM) * (DIFF_DH ** -0.5 * LOG2E), 0.0)
    lam_rows = jnp.zeros((SUBLANES, LANES), F32)
    for r, vec in enumerate((lam_q1, lam_k1, lam_q2, lam_k2)):
        lam_rows = lam_rows.at[r, :DIFF_DH].set(vec.astype(F32))
    lam_rows = lam_rows.at[4, :].set(lam_init)
    g_cols = jnp.broadcast_to(diff_subln.astype(F32)[:, None], (HEAD_DIM, tq))
    o_df = _df_attn(_aug_queries(q_df2.reshape(b, t, 2 * N_HEADS, HEAD_DIM),
                                 jnp.repeat(_split3(far[:N_HEADS]), 2, axis=0)),
                    _aug_keys(all_keys(pk_df, k_df), ones_cols, lp), _aug_values(all_keys(pv_df, v_df), lp),
                    near[:N_HEADS], lam_rows, g_cols, p_len, tq)

    kix_all = jnp.pad(_with_past(pk_ix, k_ix, l_keys), ((0, 0), (FRONT, lp - FRONT - l_keys), (0, 0))).astype(BF16)
    qix_t = jnp.transpose(q_ix.reshape(b, t, N_IDX_HEADS, IDX_DIM) * (IDX_DIM ** -0.5), (0, 2, 3, 1)).astype(BF16)
    upto = (jnp.arange(KB)[None, :] <= jnp.arange(KB)[:, None]).astype(BF16)
    o_ds = _ds_attn(qix_t, jnp.transpose(w_ix, (0, 2, 1)), kix_all,
                    _aug_queries(hd(q_ds * (scale * LOG2E)), _split3(far[N_HEADS:])),
                    _aug_keys(all_keys(pk_ds, k_ds), ones_cols, lp), _aug_values(all_keys(pv_ds, v_ds), lp),
                    near[N_HEADS:], upto, p_len, tq, l_keys, topk)

    c_all = _cumsum_seq(_with_past(plogf, logf, l_pad))
    c_keys = jnp.transpose(c_all[:, :, :l_keys], (0, 2, 1))
    o_fx = _fx_attn(_aug_queries(hd(q_fx * (scale * LOG2E)), jnp.ones((N_HEADS, N_EXTRA), F32)),
                    _aug_keys(all_keys(pk_fx, k_fx), _split3(-LOG2E * c_keys), lp),
                    _aug_values(all_keys(pv_fx, v_fx), lp), causal, p_len, tq)

    f = w_gate.shape[1]
    if pconv is None:
        pc = jnp.zeros((b, SUBLANES, f), F32)
    else:
        pc = jnp.pad(pconv.astype(F32), ((0, 0), (SUBLANES - (CONV_W - 1), 0), (0, 0)))
    cw = jnp.pad(conv_w.astype(F32), ((0, SUBLANES - CONV_W), (0, 0)))
    x_new, st = _post(x, (o_sb, o_df, o_ds, o_fx), w_out, g_ffn.reshape(1, d), w_gate, w_up,
                      w_down, cw, conv_b.reshape(1, f).astype(F32), pc, tm_post)

    rows = (hd(k_sb), hd(v_sb), k_df.reshape(b, t, N_HEADS, 2, DIFF_DH), hd(v_df), hd(k_ds), hd(v_ds),
            k_ix, hd(k_fx), hd(v_fx), logf, st[:, SUBLANES - (CONV_W - 1):])
    return x_new, rows


def _row_tile(n, cap):
    tm = min(n, cap)
    assert n % tm == 0
    return tm


def kernel(x_prompt, x_sample, cache_sb_k, cache_sb_v, cache_diff_k, cache_diff_v, cache_dsa_k, cache_dsa_v, cache_dsa_kidx, cache_fox_k, cache_fox_v, cache_fox_logf, state_ffn_conv, g_mix, w_in, b_forget, w_out, lam_q1, lam_k1, lam_q2, lam_k2, diff_subln, g_ffn, w_gate, w_up, w_down, conv_w, conv_b, t5_table, g_final):
    depth = w_in.shape[0]
    bp, tp, _ = x_prompt.shape
    bs, ts, _ = x_sample.shape
    xp, xs = x_prompt, x_sample
    p_rows, s_rows = [], []
    for l in range(depth):
        lw = (g_mix[l], _pack_w_in(w_in[l]), b_forget[l], w_out[l].astype(BF16), lam_q1[l], lam_k1[l],
              lam_q2[l], lam_k2[l], diff_subln[l], g_ffn[l], w_gate[l].astype(BF16), w_up[l].astype(BF16),
              w_down[l].astype(BF16), conv_w[l], conv_b[l])
        lam_init = 0.8 - 0.6 * math.exp(-0.3 * l)
        xp, rp = _layer(xp, None, lw, lam_init, t5_table, _row_tile(bp * tp, 512), _row_tile(tp, 256))
        s_past = (cache_sb_k[l], cache_sb_v[l], cache_diff_k[l], cache_diff_v[l], cache_dsa_k[l],
                  cache_dsa_v[l], cache_dsa_kidx[l], cache_fox_k[l], cache_fox_v[l], cache_fox_logf[l],
                  state_ffn_conv[l])
        xs, rs = _layer(xs, s_past, lw, lam_init, t5_table, _row_tile(bs * ts, 512), _row_tile(ts, 256))
        p_rows.append(rp)
        s_rows.append(rs)
    y_prompt = _final_norm(xp, g_final, _row_tile(bp * tp, 512))
    y_sample = _final_norm(xs, g_final, _row_tile(bs * ts, 512))
    stack = lambda rows, i: jnp.stack([r[i] for r in rows], axis=0)
    return ((y_prompt, y_sample) + tuple(stack(p_rows, i) for i in range(11))
            + tuple(stack(s_rows, i) for i in range(11)))
```

```python
import functools
import math

import jax
import jax.numpy as jnp
from jax import lax
from jax.experimental import pallas as pl
from jax.experimental.pallas import tpu as pltpu

F32 = jnp.float32
BF16 = jnp.bfloat16
I32 = jnp.int32
I16 = jnp.int16

N_HEADS = 4
HEAD_DIM = 64
GROUP_WIDTH = N_HEADS * HEAD_DIM
DIFF_DH = HEAD_DIM // 2
N_IDX_HEADS = 8
IDX_DIM = 64
CHUNK = 64
TOPK_MAX = 256
T5_BUCKETS = 32
T5_MAX_DIST = 128
CONV_W = 3
RMS_EPS = 1e-6
SUBLN_EPS = 1e-5

LANES = 128
SUBLANES = 8
SB_KB = 256
KB = 512
FRONT = 256
TQ_MAX = 256
N_EXTRA = 3
COL_FLAG = HEAD_DIM + N_EXTRA
V_ROWS = 80
NEG = -1e30
SB_DEAD = -110.0
HALF_BITS = 16
HALF_BIAS = 2 ** (HALF_BITS - 1)
PACKED_ROWS = 16
KEY_NEG_INF = -2139095041
LOG2E = math.log2(math.e)
VMEM_LIMIT = 56 * 1024 * 1024

COL_QIX = 9 * GROUP_WIDTH
COL_SLAB = COL_QIX + N_IDX_HEADS * IDX_DIM
SLAB_WIX = IDX_DIM
SLAB_GATE = IDX_DIM + N_IDX_HEADS
COL_FX = COL_SLAB + LANES
PROJ_PAD = COL_FX + 3 * GROUP_WIDTH


def _cparams(sem):
    return pltpu.CompilerParams(dimension_semantics=sem, vmem_limit_bytes=VMEM_LIMIT)


def _resident(shape, index_map):
    return pl.BlockSpec(shape, index_map, pipeline_mode=pl.Buffered(1))


def _full(shape):
    return _resident(shape, lambda b, i: (0,) * len(shape))


def _log_sigmoid(z):
    return jnp.minimum(z, 0.0) - jnp.log(1.0 + jnp.exp(-jnp.abs(z)))


def _dot(a, b):
    return jnp.dot(a, b, preferred_element_type=F32)


def _proj_kernel(x_ref, g_ref, w_ref, bf_ref, y_ref, lf_ref):
    x = x_ref[...]
    hn = x * lax.rsqrt(jnp.mean(x * x, axis=-1, keepdims=True) + RMS_EPS) * g_ref[...]
    y = _dot(hn.astype(BF16), w_ref[...])
    y_ref[...] = y
    lf_ref[...] = _log_sigmoid(y[:, COL_SLAB:COL_SLAB + LANES] + bf_ref[...])


def _proj(x2, g, w, bf, tm):
    n, d = x2.shape
    return pl.pallas_call(
        _proj_kernel,
        grid=(n // tm,),
        in_specs=[pl.BlockSpec((tm, d), lambda i: (i, 0)),
                  pl.BlockSpec((1, d), lambda i: (0, 0)),
                  _resident((d, PROJ_PAD), lambda i: (0, 0)),
                  pl.BlockSpec((1, LANES), lambda i: (0, 0))],
        out_specs=[pl.BlockSpec((tm, PROJ_PAD), lambda i: (i, 0)),
                   pl.BlockSpec((tm, LANES), lambda i: (i, 0))],
        out_shape=[jax.ShapeDtypeStruct((n, PROJ_PAD), F32),
                   jax.ShapeDtypeStruct((n, LANES), F32)],
        compiler_params=_cparams(("parallel",)),
        name="proj",
    )(x2, g, w, bf)


def _cumsum_kernel(x_ref, o_ref):
    nch, rows, _ = x_ref.shape
    lane = lax.broadcasted_iota(I32, (rows, LANES), 1)

    def body(c, carry):
        x = x_ref[c]
        d = 1
        while d < LANES:
            x = x + jnp.where(lane >= d, pltpu.roll(x, d, 1), 0.0)
            d *= 2
        x = x + carry
        o_ref[c] = x
        return jnp.broadcast_to(x[:, LANES - 1:LANES], (rows, LANES))

    lax.fori_loop(0, nch, body, jnp.zeros((rows, LANES), F32))


def _cumsum_seq(logf_all):
    b, l, h = logf_all.shape
    rows = b * h
    rows_pad = -(-rows // SUBLANES) * SUBLANES
    x = jnp.transpose(logf_all, (0, 2, 1)).reshape(rows, l // LANES, LANES)
    x = jnp.pad(jnp.transpose(x, (1, 0, 2)), ((0, 0), (0, rows_pad - rows), (0, 0)))
    c = pl.pallas_call(
        _cumsum_kernel,
        out_shape=jax.ShapeDtypeStruct(x.shape, F32),
        compiler_params=pltpu.CompilerParams(vmem_limit_bytes=VMEM_LIMIT),
        name="cumsum",
    )(x)
    return jnp.transpose(c[:, :rows], (1, 0, 2)).reshape(b, h, l)


def _sb_kernel(qt_ref, k_ref, vt_ref, later_ref, o_ref, run_scr, acc_scr, *, past, tq):
    q0 = pl.multiple_of(past + pl.program_id(1) * tq, SB_KB)
    run_scr[...] = jnp.zeros(run_scr.shape, F32)
    acc_scr[...] = jnp.zeros(acc_scr.shape, F32)
    later = later_ref[...]

    def block(start, strict):
        logits = [_dot(k_ref[0, h, pl.ds(start, SB_KB), :], qt_ref[0, h]) for h in range(N_HEADS)]
        log_beta, log_keep, pieces = [], [], []
        for h in range(N_HEADS):
            z = logits[h]
            ls = _log_sigmoid(z)
            lk = ls - z
            if strict is not None:
                lk = jnp.where(strict, lk, 0.0)
            hi = lk.astype(BF16)
            rem = lk - hi.astype(F32)
            mid = rem.astype(BF16)
            lo = (rem - mid.astype(F32)).astype(BF16)
            log_beta.append(ls)
            log_keep.append(lk)
            pieces.append((hi, mid, lo))
        between = [_dot(later, pieces[h][0]) + _dot(later, pieces[h][1]) + _dot(later, pieces[h][2])
                   for h in range(N_HEADS)]
        weights = []
        for h in range(N_HEADS):
            a = jnp.exp(log_beta[h] + between[h] + run_scr[h])
            if strict is not None:
                a = jnp.where(strict, a, 0.0)
            weights.append(a.astype(BF16))
        for h in range(N_HEADS):
            acc_scr[h] = acc_scr[h] + _dot(vt_ref[0, h, :, pl.ds(start, SB_KB)], weights[h])
            run_scr[h] = run_scr[h] + jnp.sum(log_keep[h], axis=0, keepdims=True)

    block(q0, lax.broadcasted_iota(I32, (SB_KB, tq), 0) < lax.broadcasted_iota(I32, (SB_KB, tq), 1))

    def cond(carry):
        return (carry[0] >= 0) & (carry[1] > SB_DEAD)

    def body(carry):
        block(pl.multiple_of(carry[0], SB_KB), None)
        return carry[0] - SB_KB, jnp.max(run_scr[...])

    lax.while_loop(cond, body, (q0 - SB_KB, jnp.max(run_scr[...])))
    for h in range(N_HEADS):
        o_ref[0, h * HEAD_DIM:(h + 1) * HEAD_DIM, :] = acc_scr[h]


def _sb_attn(qt, k, vt, later, past, tq):
    b, _, _, t = qt.shape
    l_sb = k.shape[2]
    return pl.pallas_call(
        functools.partial(_sb_kernel, past=past, tq=tq),
        grid=(b, t // tq),
        in_specs=[pl.BlockSpec((1, N_HEADS, HEAD_DIM, tq), lambda b, i: (b, 0, 0, i)),
                  _resident((1, N_HEADS, l_sb, HEAD_DIM), lambda b, i: (b, 0, 0, 0)),
                  _resident((1, N_HEADS, HEAD_DIM, l_sb), lambda b, i: (b, 0, 0, 0)),
                  _full(later.shape)],
        out_specs=pl.BlockSpec((1, GROUP_WIDTH, tq), lambda b, i: (b, 0, i)),
        out_shape=jax.ShapeDtypeStruct((b, GROUP_WIDTH, t), F32),
        scratch_shapes=[pltpu.VMEM((N_HEADS, 1, tq), F32), pltpu.VMEM((N_HEADS, HEAD_DIM, tq), F32)],
        compiler_params=_cparams(("parallel", "parallel")),
        name="stick_breaking",
    )(qt, k, vt, later)


def _n_far_blocks(q0):
    return jnp.maximum(q0 - FRONT + KB - 1, 0) // KB


def _far_start(q0, j):
    return pl.multiple_of(q0 - KB * (j + 1), FRONT)


def _flash_tile(n_maps, head_of, qa_ref, ka_ref, va_ref, state, q0, near_add, far_add):
    m_scr, acc_scr, s_scr, smax_scr = state
    m_scr[...] = jnp.full(m_scr.shape, NEG, F32)
    acc_scr[...] = jnp.zeros(acc_scr.shape, F32)

    def start_of(i):
        return pl.multiple_of(q0 - KB * i, FRONT)

    def scores(mp, start, add):
        s = _dot(ka_ref[0, head_of(mp), pl.ds(start, KB), :], qa_ref[0, mp])
        if add is not None:
            s = s + add(mp)
        s_scr[mp] = s
        smax_scr[mp] = jnp.max(s, axis=0, keepdims=True)

    def absorb(mp, start):
        m_old = m_scr[mp]
        m_new = jnp.maximum(m_old, smax_scr[mp])
        p = jnp.exp2(s_scr[mp] - m_new).astype(BF16)
        acc_scr[mp] = (acc_scr[mp] * jnp.exp2(m_old - m_new)
                       + _dot(va_ref[0, head_of(mp), :, pl.ds(start, KB)], p))
        m_scr[mp] = m_new

    for mp in range(n_maps):
        scores(mp, start_of(0), near_add)

    def body(i, c):
        nxt = start_of(i + 1)
        for mp in range(n_maps):
            absorb(mp, start_of(i))
            scores(mp, nxt, None if far_add is None else (lambda mp_: far_add(nxt)))
        return c

    n_far = _n_far_blocks(q0)
    lax.fori_loop(0, n_far, body, 0)
    for mp in range(n_maps):
        absorb(mp, start_of(n_far))
    return acc_scr


def _normalized(acc_scr, mp):
    return acc_scr[mp, 0:HEAD_DIM, :] / acc_scr[mp, HEAD_DIM:HEAD_DIM + 1, :]


def _flash_specs(n_maps, tq, lp):
    qa_spec = pl.BlockSpec((1, n_maps, LANES, tq), lambda b, i: (b, 0, 0, i))
    ka_spec = _resident((1, N_HEADS, lp, LANES), lambda b, i: (b, 0, 0, 0))
    va_spec = _resident((1, N_HEADS, V_ROWS, lp), lambda b, i: (b, 0, 0, 0))
    o_spec = pl.BlockSpec((1, GROUP_WIDTH, tq), lambda b, i: (b, 0, i))
    state = [pltpu.VMEM((n_maps, 1, tq), F32), pltpu.VMEM((n_maps, V_ROWS, tq), F32),
             pltpu.VMEM((n_maps, KB, tq), F32), pltpu.VMEM((n_maps, 1, tq), F32)]
    return qa_spec, ka_spec, va_spec, o_spec, state


def _fx_kernel(qa_ref, ka_ref, va_ref, causal_ref, o_ref, *state, past, tq):
    q0 = past + pl.program_id(1) * tq
    acc_scr = _flash_tile(N_HEADS, lambda mp: mp, qa_ref, ka_ref, va_ref, state, q0,
                          lambda mp: causal_ref[...], None)
    for h in range(N_HEADS):
        o_ref[0, h * HEAD_DIM:(h + 1) * HEAD_DIM, :] = _normalized(acc_scr, h)


def _fx_attn(qa, ka, va, causal, past, tq):
    b, _, _, t = qa.shape
    qa_spec, ka_spec, va_spec, o_spec, scratch = _flash_specs(N_HEADS, tq, ka.shape[2])
    return pl.pallas_call(
        functools.partial(_fx_kernel, past=past, tq=tq),
        grid=(b, t // tq),
        in_specs=[qa_spec, ka_spec, va_spec, _full(causal.shape)],
        out_specs=o_spec,
        out_shape=jax.ShapeDtypeStruct((b, GROUP_WIDTH, t), F32),
        scratch_shapes=scratch,
        compiler_params=_cparams(("parallel", "parallel")),
        name="forgetting",
    )(qa, ka, va, causal)


def _df_kernel(qa_ref, ka_ref, va_ref, near_ref, lam_ref, g_ref, o_ref, *state, past, tq):
    q0 = past + pl.program_id(1) * tq
    acc_scr = _flash_tile(2 * N_HEADS, lambda mp: mp // 2, qa_ref, ka_ref, va_ref, state, q0,
                          lambda mp: near_ref[mp // 2], None)
    lam_init = lam_ref[4:5, 0:1]
    lam = (jnp.exp(jnp.sum(lam_ref[0:1, :] * lam_ref[1:2, :], axis=-1, keepdims=True))
           - jnp.exp(jnp.sum(lam_ref[2:3, :] * lam_ref[3:4, :], axis=-1, keepdims=True)) + lam_init)
    for h in range(N_HEADS):
        o = _normalized(acc_scr, 2 * h) - lam * _normalized(acc_scr, 2 * h + 1)
        o = o * lax.rsqrt(jnp.mean(o * o, axis=0, keepdims=True) + SUBLN_EPS) * g_ref[...]
        o_ref[0, h * HEAD_DIM:(h + 1) * HEAD_DIM, :] = o * (1.0 - lam_init)


def _df_attn(qa, ka, va, near, lam_rows, g_cols, past, tq):
    b, _, _, t = qa.shape
    qa_spec, ka_spec, va_spec, o_spec, scratch = _flash_specs(2 * N_HEADS, tq, ka.shape[2])
    return pl.pallas_call(
        functools.partial(_df_kernel, past=past, tq=tq),
        grid=(b, t // tq),
        in_specs=[qa_spec, ka_spec, va_spec, _full(near.shape), _full(lam_rows.shape), _full(g_cols.shape)],
        out_specs=o_spec,
        out_shape=jax.ShapeDtypeStruct((b, GROUP_WIDTH, t), F32),
        scratch_shapes=scratch,
        compiler_params=_cparams(("parallel", "parallel")),
        name="differential",
    )(qa, ka, va, near, lam_rows, g_cols)


def _ds_kernel(qix_ref, wix_ref, kix_ref, qa_ref, ka_ref, va_ref, near_ref, upto_ref, o_ref,
               key_scr, k16_scr, msk_scr, *state, past, tq, l_keys, topk):
    q0 = past + pl.program_id(1) * tq
    near0 = pl.multiple_of(q0, FRONT)
    n_far = _n_far_blocks(q0)
    wix = wix_ref[0] * (N_IDX_HEADS ** -0.5)

    def score_block(start, near):
        kix = kix_ref[0, pl.ds(start, KB), :]
        sc = jnp.zeros((KB, tq), F32)
        for hh in range(N_IDX_HEADS):
            sc = sc + wix[hh:hh + 1, :] * jnp.maximum(_dot(kix, qix_ref[0, hh]), 0.0)
        kidx = start + lax.broadcasted_iota(I32, (KB, tq), 0)
        ok = (kidx >= FRONT) & (kidx < FRONT + l_keys)
        if near:
            ok = ok & (near_ref[0] > 0.5 * NEG)
        bits = lax.bitcast_convert_type(jnp.where(ok, sc, -jnp.inf), I32)
        key = bits ^ ((bits >> 31) & 0x7FFFFFFF)
        key_scr[pl.ds(start, KB), :] = key
        k16_scr[pl.ds(start, KB), :] = (key >> HALF_BITS).astype(I16)

    def each_block(fn):
        fn(near0)

        def far(j, c):
            fn(_far_start(q0, j))
            return c

        lax.fori_loop(0, n_far, far, 0)

    score_block(near0, True)

    def score_far(j, c):
        score_block(_far_start(q0, j), False)
        return c

    lax.fori_loop(0, n_far, score_far, 0)

    def count16(cand):
        cand16 = cand.astype(I16)

        def blk(start):
            ones = jnp.where(k16_scr[pl.ds(start, KB), :] >= cand16, jnp.ones((), BF16), jnp.zeros((), BF16))
            parts = [ones[r:r + PACKED_ROWS] for r in range(0, KB, PACKED_ROWS)]
            while len(parts) > 1:
                parts = [parts[r] + parts[r + 1] for r in range(0, len(parts), 2)]
            return parts[0].astype(F32)

        acc = lax.fori_loop(0, n_far, lambda j, c: c + blk(_far_start(q0, j)), blk(near0))
        return jnp.sum(acc, axis=0, keepdims=True)

    def greedy16(n_all):
        def bit_body(i, carry):
            u, n_u = carry
            cand = u | lax.shift_left(jnp.int32(1), HALF_BITS - 1 - i)
            n_cand = count16(cand - HALF_BIAS)
            take = n_cand >= topk
            return jnp.where(take, cand, u), jnp.where(take, n_cand, n_u)
        return lax.fori_loop(0, HALF_BITS, bit_body, (jnp.zeros((1, tq), I32), n_all))

    n_scratch = ((n_far + 1) * KB).astype(F32)
    hi_u, n_hi = greedy16(jnp.full((1, tq), n_scratch, F32))
    thr_hi = hi_u - HALF_BIAS

    def refine_block(start):
        key = key_scr[pl.ds(start, KB), :]
        hi = key >> HALF_BITS
        low = jnp.where(hi > thr_hi, HALF_BIAS - 1,
                        jnp.where(hi < thr_hi, -HALF_BIAS, (key & (2 * HALF_BIAS - 1)) - HALF_BIAS))
        k16_scr[pl.ds(start, KB), :] = low.astype(I16)

    each_block(refine_block)
    lo_u, n_ge = greedy16(n_hi)
    thr = thr_hi * (2 * HALF_BIAS) + lo_u

    tied = jnp.max(n_ge) > topk

    @pl.when(jnp.logical_not(tied))
    def _():
        def plain_block(start):
            key = key_scr[pl.ds(start, KB), :]
            msk_scr[pl.ds(start, KB), :] = jnp.where((key >= thr) & (key > KEY_NEG_INF), 0.0, NEG)
        each_block(plain_block)

    @pl.when(tied)
    def _():
        def above(start):
            return jnp.sum((key_scr[pl.ds(start, KB), :] > thr).astype(I32), axis=0, keepdims=True)

        n_gt = lax.fori_loop(0, n_far, lambda j, c: c + above(_far_start(q0, j)), above(near0))
        need = (topk - n_gt).astype(F32)

        def mask_block(start, seen):
            key = key_scr[pl.ds(start, KB), :]
            tie = key == thr
            rank = _dot(upto_ref[...], tie.astype(BF16)) + seen
            sel = (key > KEY_NEG_INF) & ((key > thr) | (tie & (rank <= need)))
            msk_scr[pl.ds(start, KB), :] = jnp.where(sel, 0.0, NEG)
            return seen + jnp.sum(tie.astype(F32), axis=0, keepdims=True)

        seen = lax.fori_loop(0, n_far, lambda i, c: mask_block(_far_start(q0, n_far - 1 - i), c),
                             jnp.zeros((1, tq), F32))
        mask_block(near0, seen)

    acc_scr = _flash_tile(N_HEADS, lambda mp: mp, qa_ref, ka_ref, va_ref, state, q0,
                          lambda mp: near_ref[mp] + msk_scr[pl.ds(near0, KB), :],
                          lambda start: msk_scr[pl.ds(start, KB), :])
    for h in range(N_HEADS):
        o_ref[0, h * HEAD_DIM:(h + 1) * HEAD_DIM, :] = _normalized(acc_scr, h)


def _ds_attn(qix, wix, kix, qa, ka, va, near, upto, past, tq, l_keys, topk):
    b, _, _, t = qa.shape
    lp = ka.shape[2]
    qa_spec, ka_spec, va_spec, o_spec, scratch = _flash_specs(N_HEADS, tq, lp)
    return pl.pallas_call(
        functools.partial(_ds_kernel, past=past, tq=tq, l_keys=l_keys, topk=topk),
        grid=(b, t // tq),
        in_specs=[pl.BlockSpec((1, N_IDX_HEADS, IDX_DIM, tq), lambda b, i: (b, 0, 0, i)),
                  pl.BlockSpec((1, N_IDX_HEADS, tq), lambda b, i: (b, 0, i)),
                  _resident((1, lp, IDX_DIM), lambda b, i: (b, 0, 0)),
                  qa_spec, ka_spec, va_spec, _full(near.shape), _full(upto.shape)],
        out_specs=o_spec,
        out_shape=jax.ShapeDtypeStruct((b, GROUP_WIDTH, t), F32),
        scratch_shapes=[pltpu.VMEM((lp, tq), I32), pltpu.VMEM((lp, tq), I16), pltpu.VMEM((lp, tq), F32)] + scratch,
        compiler_params=_cparams(("parallel", "parallel")),
        name="indexed_sparse",
    )(qix, wix, kix, qa, ka, va, near, upto)


def _post_kernel(x_ref, o0_ref, o1_ref, o2_ref, o3_ref, wo_ref, g_ref, wg_ref, wu_ref, wd_ref,
                 cw_ref, cb_ref, pc_ref, xo_ref, st_ref, uext, *, tm):
    t = pl.program_id(1)

    @pl.when(t == 0)
    def _():
        uext[0:SUBLANES, :] = pc_ref[0]

    @pl.when(t > 0)
    def _():
        uext[0:SUBLANES, :] = uext[tm:tm + SUBLANES, :]

    x = x_ref[0]
    for gi, ot_ref in enumerate((o0_ref, o1_ref, o2_ref, o3_ref)):
        x = x + lax.dot_general(ot_ref[0].astype(BF16), wo_ref[gi * GROUP_WIDTH:(gi + 1) * GROUP_WIDTH, :],
                                (((0,), (0,)), ((), ())), preferred_element_type=F32)
    hn = (x * lax.rsqrt(jnp.mean(x * x, axis=-1, keepdims=True) + RMS_EPS) * g_ref[...]).astype(BF16)
    u = _dot(hn, wg_ref[...])
    uext[SUBLANES:SUBLANES + tm, :] = u
    conv = cb_ref[...]
    conv = conv + cw_ref[0:1, :] * uext[SUBLANES - 2:SUBLANES - 2 + tm, :]
    conv = conv + cw_ref[1:2, :] * uext[SUBLANES - 1:SUBLANES - 1 + tm, :]
    conv = conv + cw_ref[2:3, :] * u
    hidden = conv * (1.0 / (1.0 + jnp.exp(-conv))) * _dot(hn, wu_ref[...])
    xo_ref[0] = x + _dot(hidden.astype(BF16), wd_ref[...])
    st_ref[0] = uext[tm:tm + SUBLANES, :]


def _post(x, outs, wo, g, wg, wu, wd, cw, cb, pconv, tm):
    b, t, d = x.shape
    f = wg.shape[1]
    row = lambda w: pl.BlockSpec((1, tm, w), lambda bi, ti: (bi, ti, 0))
    const = lambda shape: _resident(shape, lambda bi, ti: (0,) * len(shape))
    return pl.pallas_call(
        functools.partial(_post_kernel, tm=tm),
        grid=(b, t // tm),
        in_specs=[row(d)] + [pl.BlockSpec((1, GROUP_WIDTH, tm), lambda bi, ti: (bi, 0, ti))] * 4 + [
            const(wo.shape), const(g.shape), const(wg.shape), const(wu.shape), const(wd.shape),
            const(cw.shape), const(cb.shape),
            pl.BlockSpec((1, SUBLANES, f), lambda bi, ti: (bi, 0, 0))],
        out_specs=[row(d), pl.BlockSpec((1, SUBLANES, f), lambda bi, ti: (bi, 0, 0))],
        out_shape=[jax.ShapeDtypeStruct((b, t, d), F32),
                   jax.ShapeDtypeStruct((b, SUBLANES, f), F32)],
        scratch_shapes=[pltpu.VMEM((tm + SUBLANES, f), F32)],
        compiler_params=_cparams(("parallel", "arbitrary")),
        name="out_proj_mlp",
    )(x, *outs, wo, g, wg, wu, wd, cw, cb, pconv)


def _final_norm_kernel(x_ref, g_ref, o_ref):
    x = x_ref[...]
    o_ref[...] = x * lax.rsqrt(jnp.mean(x * x, axis=-1, keepdims=True) + RMS_EPS) * g_ref[...]


def _final_norm(x, g, tm):
    b, t, d = x.shape
    x2 = x.reshape(b * t, d)
    y = pl.pallas_call(
        _final_norm_kernel,
        grid=(b * t // tm,),
        in_specs=[pl.BlockSpec((tm, d), lambda i: (i, 0)), pl.BlockSpec((1, d), lambda i: (0, 0))],
        out_specs=pl.BlockSpec((tm, d), lambda i: (i, 0)),
        out_shape=jax.ShapeDtypeStruct(x2.shape, F32),
        compiler_params=_cparams(("parallel",)),
        name="final_norm",
    )(x2, g.reshape(1, d))
    return y.reshape(b, t, d)


def _t5_bucket(rel):
    half = T5_BUCKETS // 2
    max_exact = half // 2
    base = jnp.where(rel > 0, half, 0)
    n = jnp.abs(rel)
    nf = jnp.maximum(n, 1).astype(F32)
    large = max_exact + (jnp.log(nf / max_exact) / math.log(T5_MAX_DIST / max_exact)
                         * (half - max_exact)).astype(I32)
    large = jnp.minimum(large, half - 1)
    return base + jnp.where(n < max_exact, n, large)


def _bf16_part(x):
    bits = lax.bitcast_convert_type(x, jnp.uint32) & jnp.uint32(0xFFFF0000)
    return lax.bitcast_convert_type(bits, F32)


def _split3(x):
    hi = _bf16_part(x)
    mid = _bf16_part(x - hi)
    lo = _bf16_part(x - hi - mid)
    return jnp.stack([hi, mid, lo], axis=-1)


def _near_tables(t5_table, tq):
    kk = jnp.arange(KB, dtype=I32)[:, None] - FRONT
    qq = jnp.arange(tq, dtype=I32)[None, :]
    tab = t5_table.astype(F32).T
    far = tab[:, _t5_bucket(jnp.int32(-T5_MAX_DIST))]
    one_hot = (_t5_bucket(kk - qq)[:, :, None] == jnp.arange(T5_BUCKETS, dtype=I32)).astype(F32)
    bias = jnp.einsum("kqb,hb->hkq", one_hot, tab, precision=lax.Precision.HIGHEST)
    chunk_ok = (kk // CHUNK) <= (qq // CHUNK)
    near = jnp.where(chunk_ok[None], (bias - far[:, None, None]) * LOG2E, NEG)
    causal = jnp.where(kk <= qq, 0.0, NEG).astype(F32)
    return near, far * LOG2E, causal


def _pack_w_in(w_in):
    d = w_in.shape[0]
    fx0 = COL_SLAB + IDX_DIM + N_IDX_HEADS
    pad = jnp.zeros((d, LANES - SLAB_GATE - N_HEADS), w_in.dtype)
    return jnp.concatenate([w_in[:, :fx0], w_in[:, fx0 + 3 * GROUP_WIDTH:], pad,
                            w_in[:, fx0:fx0 + 3 * GROUP_WIDTH]], axis=1).astype(BF16)


def _with_past(past, new, l_pad):
    b, t = new.shape[:2]
    new = new.reshape(b, t, -1)
    if past is not None:
        new = jnp.concatenate([past.reshape(b, past.shape[1], -1).astype(F32), new], axis=1)
    return jnp.pad(new, ((0, 0), (0, l_pad - new.shape[1]), (0, 0)))


def _aug_queries(q, rows):
    qt = jnp.pad(jnp.transpose(q, (0, 2, 3, 1)), ((0, 0), (0, 0), (0, LANES - HEAD_DIM), (0, 0)))
    row = jnp.arange(LANES)[None, None, :, None]
    for e in range(N_EXTRA):
        qt = jnp.where(row == HEAD_DIM + e, rows[None, :, e, None, None], qt)
    return jnp.where(row == COL_FLAG, 1.0, qt).astype(BF16)


def _aug_keys(k, cols, lp):
    l = k.shape[1]
    lane = jnp.arange(LANES)
    body = jnp.pad(k, ((0, 0), (0, 0), (0, 0), (0, LANES - HEAD_DIM)))
    for e in range(N_EXTRA):
        body = jnp.where(lane == HEAD_DIM + e, cols[..., e:e + 1], body)
    body = jnp.pad(body, ((0, 0), (FRONT, lp - FRONT - l), (0, 0), (0, 0)))
    row = jnp.arange(lp)[None, :, None, None]
    is_pad = (row < FRONT) | (row >= FRONT + l)
    full = jnp.where(is_pad & (lane == COL_FLAG), NEG, body)
    return jnp.transpose(full, (0, 2, 1, 3)).astype(BF16)


def _aug_values(v, lp):
    l = v.shape[1]
    vt = jnp.pad(jnp.transpose(v, (0, 2, 3, 1)),
                 ((0, 0), (0, 0), (0, V_ROWS - HEAD_DIM), (FRONT, lp - FRONT - l)))
    row = jnp.arange(V_ROWS)[None, None, :, None]
    return jnp.where(row == HEAD_DIM, 1.0, vt).astype(BF16)


def _layer(x, past, lw, lam_init, t5_table, tm_proj, tm_post):
    (g_mix, w_in_p, b_forget, w_out, lam_q1, lam_k1, lam_q2, lam_k2, diff_subln,
     g_ffn, w_gate, w_up, w_down, conv_w, conv_b) = lw
    b, t, d = x.shape
    p_len = 0 if past is None else past[0].shape[1]
    l_keys = p_len + t
    l_pad = -(-l_keys // LANES) * LANES
    topk = min(TOPK_MAX, l_keys // 4)
    tq = min(t, TQ_MAX)
    assert p_len % FRONT == 0 and (t % TQ_MAX == 0 or t <= LANES) and FRONT == SB_KB == TQ_MAX
    lp = -(-max(FRONT + l_keys, p_len + t - tq + KB) // LANES) * LANES
    l_sb = p_len + t - tq + SB_KB
    (pk_sb, pv_sb, pk_df, pv_df, pk_ds, pv_ds, pk_ix, pk_fx, pv_fx, plogf, pconv) = (
        past if past is not None else (None,) * 11)

    bf = jnp.zeros((1, LANES), F32).at[0, SLAB_GATE:SLAB_GATE + N_HEADS].set(b_forget)
    y, lf = _proj(x.reshape(b * t, d), g_mix.reshape(1, d), w_in_p, bf, tm_proj)
    y = y.reshape(b, t, PROJ_PAD)
    grp = lambda i: y[:, :, i * GROUP_WIDTH:(i + 1) * GROUP_WIDTH]
    q_sb, k_sb, v_sb, q_df, k_df, v_df, q_ds, k_ds, v_ds = [grp(i) for i in range(9)]
    q_ix = y[:, :, COL_QIX:COL_SLAB]
    slab = y[:, :, COL_SLAB:COL_FX]
    k_ix = slab[:, :, :IDX_DIM]
    w_ix = slab[:, :, SLAB_WIX:SLAB_WIX + N_IDX_HEADS]
    q_fx, k_fx, v_fx = [y[:, :, COL_FX + i * GROUP_WIDTH:COL_FX + (i + 1) * GROUP_WIDTH] for i in range(3)]
    logf = lf.reshape(b, t, LANES)[:, :, SLAB_GATE:SLAB_GATE + N_HEADS]

    scale = HEAD_DIM ** -0.5
    hd = lambda a: a.reshape(b, -1, N_HEADS, HEAD_DIM)
    all_keys = lambda pst, new: hd(_with_past(pst, new, l_keys))
    near, far, causal = _near_tables(t5_table, tq)
    ones_cols = jnp.ones((b, l_keys, N_HEADS, N_EXTRA), F32)

    k_sb_all = jnp.transpose(hd(_with_past(pk_sb, k_sb, l_sb)), (0, 2, 1, 3)).astype(BF16)
    vt_sb_all = jnp.transpose(hd(_with_past(pv_sb, v_sb, l_sb)), (0, 2, 3, 1)).astype(BF16)
    later = (jnp.arange(SB_KB)[None, :] > jnp.arange(SB_KB)[:, None]).astype(BF16)
    o_sb = _sb_attn(jnp.transpose(hd(q_sb * scale), (0, 2, 3, 1)).astype(BF16), k_sb_all, vt_sb_all, later,
                    p_len, tq)

    half = jnp.arange(HEAD_DIM) < DIFF_DH
    q_df2 = jnp.where(jnp.stack([half, ~half])[None, None, None],
                      q_df.reshape(b, t, N_HEADS, 1, HEAD_DIM) * (DIFF_DH ** -0.5 * LOG2E), 0.0)
    lam_rows = jnp.zeros((SUBLANES, LANES), F32)
    for r, vec in enumerate((lam_q1, lam_k1, lam_q2, lam_k2)):
        lam_rows = lam_rows.at[r, :DIFF_DH].set(vec.astype(F32))
    lam_rows = lam_rows.at[4, :].set(lam_init)
    g_cols = jnp.broadcast_to(diff_subln.astype(F32)[:, None], (HEAD_DIM, tq))
    o_df = _df_attn(_aug_queries(q_df2.reshape(b, t, 2 * N_HEADS, HEAD_DIM),
                                 jnp.repeat(_split3(far[:N_HEADS]), 2, axis=0)),
                    _aug_keys(all_keys(pk_df, k_df), ones_cols, lp), _aug_values(all_keys(pv_df, v_df), lp),
                    near[:N_HEADS], lam_rows, g_cols, p_len, tq)

    kix_all = jnp.pad(_with_past(pk_ix, k_ix, l_keys), ((0, 0), (FRONT, lp - FRONT - l_keys), (0, 0))).astype(BF16)
    qix_t = jnp.transpose(q_ix.reshape(b, t, N_IDX_HEADS, IDX_DIM) * (IDX_DIM ** -0.5), (0, 2, 3, 1)).astype(BF16)
    upto = (jnp.arange(KB)[None, :] <= jnp.arange(KB)[:, None]).astype(BF16)
    o_ds = _ds_attn(qix_t, jnp.transpose(w_ix, (0, 2, 1)), kix_all,
                    _aug_queries(hd(q_ds * (scale * LOG2E)), _split3(far[N_HEADS:])),
                    _aug_keys(all_keys(pk_ds, k_ds), ones_cols, lp), _aug_values(all_keys(pv_ds, v_ds), lp),
                    near[N_HEADS:], upto, p_len, tq, l_keys, topk)

    c_all = _cumsum_seq(_with_past(plogf, logf, l_pad))
    c_keys = jnp.transpose(c_all[:, :, :l_keys], (0, 2, 1))
    o_fx = _fx_attn(_aug_queries(hd(q_fx * (scale * LOG2E)), jnp.ones((N_HEADS, N_EXTRA), F32)),
                    _aug_keys(all_keys(pk_fx, k_fx), _split3(-LOG2E * c_keys), lp),
                    _aug_values(all_keys(pv_fx, v_fx), lp), causal, p_len, tq)

    f = w_gate.shape[1]
    if pconv is None:
        pc = jnp.zeros((b, SUBLANES, f), F32)
    else:
        pc = jnp.pad(pconv.astype(F32), ((0, 0), (SUBLANES - (CONV_W - 1), 0), (0, 0)))
    cw = jnp.pad(conv_w.astype(F32), ((0, SUBLANES - CONV_W), (0, 0)))
    x_new, st = _post(x, (o_sb, o_df, o_ds, o_fx), w_out, g_ffn.reshape(1, d), w_gate, w_up,
                      w_down, cw, conv_b.reshape(1, f).astype(F32), pc, tm_post)

    rows = (hd(k_sb), hd(v_sb), k_df.reshape(b, t, N_HEADS, 2, DIFF_DH), hd(v_df), hd(k_ds), hd(v_ds),
            k_ix, hd(k_fx), hd(v_fx), logf, st[:, SUBLANES - (CONV_W - 1):])
    return x_new, rows


def _row_tile(n, cap):
    tm = min(n, cap)
    while n % tm:
        tm -= SUBLANES
    assert tm > 0
    return tm


def kernel(x_prompt, x_sample, cache_sb_k, cache_sb_v, cache_diff_k, cache_diff_v, cache_dsa_k, cache_dsa_v, cache_dsa_kidx, cache_fox_k, cache_fox_v, cache_fox_logf, state_ffn_conv, g_mix, w_in, b_forget, w_out, lam_q1, lam_k1, lam_q2, lam_k2, diff_subln, g_ffn, w_gate, w_up, w_down, conv_w, conv_b, t5_table, g_final):
    depth = w_in.shape[0]
    bp, tp, _ = x_prompt.shape
    bs, ts, _ = x_sample.shape
    xp, xs = x_prompt, x_sample
    p_rows, s_rows = [], []
    for l in range(depth):
        lw = (g_mix[l], _pack_w_in(w_in[l]), b_forget[l], w_out[l].astype(BF16), lam_q1[l], lam_k1[l],
              lam_q2[l], lam_k2[l], diff_subln[l], g_ffn[l], w_gate[l].astype(BF16), w_up[l].astype(BF16),
              w_down[l].astype(BF16), conv_w[l], conv_b[l])
        lam_init = 0.8 - 0.6 * math.exp(-0.3 * l)
        xp, rp = _layer(xp, None, lw, lam_init, t5_table, _row_tile(bp * tp, 1024), _row_tile(tp, 512))
        s_past = (cache_sb_k[l], cache_sb_v[l], cache_diff_k[l], cache_diff_v[l], cache_dsa_k[l],
                  cache_dsa_v[l], cache_dsa_kidx[l], cache_fox_k[l], cache_fox_v[l], cache_fox_logf[l],
                  state_ffn_conv[l])
        xs, rs = _layer(xs, s_past, lw, lam_init, t5_table, _row_tile(bs * ts, 512), _row_tile(ts, 256))
        p_rows.append(rp)
        s_rows.append(rs)
    y_prompt = _final_norm(xp, g_final, _row_tile(bp * tp, 512))
    y_sample = _final_norm(xs, g_final, _row_tile(bs * ts, 512))
    stack = lambda rows, i: jnp.stack([r[i] for r in rows], axis=0)
    return ((y_prompt, y_sample) + tuple(stack(p_rows, i) for i in range(11))
            + tuple(stack(s_rows, i) for i in range(11)))
```

```python
import functools
import math

import jax
import jax.numpy as jnp
from jax import lax
from jax.experimental import pallas as pl
from jax.experimental.pallas import tpu as pltpu

F32 = jnp.float32
BF16 = jnp.bfloat16
I32 = jnp.int32
I16 = jnp.int16

N_HEADS = 4
HEAD_DIM = 64
GROUP_WIDTH = N_HEADS * HEAD_DIM
DIFF_DH = HEAD_DIM // 2
N_IDX_HEADS = 8
IDX_DIM = 64
CHUNK = 64
TOPK_MAX = 256
T5_BUCKETS = 32
T5_MAX_DIST = 128
CONV_W = 3
RMS_EPS = 1e-6
SUBLN_EPS = 1e-5

LANES = 128
SUBLANES = 8
SB_KB = 256
KB = 512
FRONT = 256
TQ_MAX = 256
N_EXTRA = 3
COL_FLAG = HEAD_DIM + N_EXTRA
V_ROWS = 80
NEG = -1e30
SB_DEAD = -110.0
HALF_BITS = 16
HALF_BIAS = 2 ** (HALF_BITS - 1)
PACKED_ROWS = 16
KEY_NEG_INF = -2139095041
LOG2E = math.log2(math.e)
VMEM_LIMIT = 56 * 1024 * 1024
SMALL_BLOCK_ELEMS = 1024 * 1024

COL_QIX = 9 * GROUP_WIDTH
COL_SLAB = COL_QIX + N_IDX_HEADS * IDX_DIM
SLAB_WIX = IDX_DIM
SLAB_GATE = IDX_DIM + N_IDX_HEADS
COL_FX = COL_SLAB + LANES
PROJ_PAD = COL_FX + 3 * GROUP_WIDTH


def _cparams(sem):
    return pltpu.CompilerParams(dimension_semantics=sem, vmem_limit_bytes=VMEM_LIMIT)


def _resident(shape, index_map):
    if math.prod(shape) <= SMALL_BLOCK_ELEMS:
        return pl.BlockSpec(shape, index_map)
    return pl.BlockSpec(shape, index_map, pipeline_mode=pl.Buffered(1))


def _full(shape):
    return _resident(shape, lambda b, i: (0,) * len(shape))


def _log_sigmoid(z):
    return jnp.minimum(z, 0.0) - jnp.log(1.0 + jnp.exp(-jnp.abs(z)))


def _dot(a, b):
    return jnp.dot(a, b, preferred_element_type=F32)


def _proj_kernel(x_ref, g_ref, w_ref, bf_ref, y_ref, lf_ref):
    x = x_ref[...]
    hn = x * lax.rsqrt(jnp.mean(x * x, axis=-1, keepdims=True) + RMS_EPS) * g_ref[...]
    y = _dot(hn.astype(BF16), w_ref[...])
    y_ref[...] = y
    lf_ref[...] = _log_sigmoid(y[:, COL_SLAB:COL_SLAB + LANES] + bf_ref[...])


def _proj(x2, g, w, bf, tm):
    n, d = x2.shape
    return pl.pallas_call(
        _proj_kernel,
        grid=(n // tm,),
        in_specs=[pl.BlockSpec((tm, d), lambda i: (i, 0)),
                  pl.BlockSpec((1, d), lambda i: (0, 0)),
                  _resident((d, PROJ_PAD), lambda i: (0, 0)),
                  pl.BlockSpec((1, LANES), lambda i: (0, 0))],
        out_specs=[pl.BlockSpec((tm, PROJ_PAD), lambda i: (i, 0)),
                   pl.BlockSpec((tm, LANES), lambda i: (i, 0))],
        out_shape=[jax.ShapeDtypeStruct((n, PROJ_PAD), F32),
                   jax.ShapeDtypeStruct((n, LANES), F32)],
        compiler_params=_cparams(("parallel",)),
        name="proj",
    )(x2, g, w, bf)


def _cumsum_kernel(x_ref, o_ref):
    nch, rows, _ = x_ref.shape
    lane = lax.broadcasted_iota(I32, (rows, LANES), 1)

    def body(c, carry):
        x = x_ref[c]
        d = 1
        while d < LANES:
            x = x + jnp.where(lane >= d, pltpu.roll(x, d, 1), 0.0)
            d *= 2
        x = x + carry
        o_ref[c] = x
        return jnp.broadcast_to(x[:, LANES - 1:LANES], (rows, LANES))

    lax.fori_loop(0, nch, body, jnp.zeros((rows, LANES), F32))


def _cumsum_seq(logf_all):
    b, l, h = logf_all.shape
    rows = b * h
    rows_pad = -(-rows // SUBLANES) * SUBLANES
    x = jnp.transpose(logf_all, (0, 2, 1)).reshape(rows, l // LANES, LANES)
    x = jnp.pad(jnp.transpose(x, (1, 0, 2)), ((0, 0), (0, rows_pad - rows), (0, 0)))
    c = pl.pallas_call(
        _cumsum_kernel,
        out_shape=jax.ShapeDtypeStruct(x.shape, F32),
        compiler_params=pltpu.CompilerParams(vmem_limit_bytes=VMEM_LIMIT),
        name="cumsum",
    )(x)
    return jnp.transpose(c[:, :rows], (1, 0, 2)).reshape(b, h, l)


def _sb_kernel(qt_ref, k_ref, vt_ref, later_ref, o_ref, run_scr, acc_scr, *, past, tq):
    q0 = pl.multiple_of(past + pl.program_id(1) * tq, SB_KB)
    run_scr[...] = jnp.zeros(run_scr.shape, F32)
    acc_scr[...] = jnp.zeros(acc_scr.shape, F32)
    later = later_ref[...]

    def block(start, strict):
        logits = [_dot(k_ref[0, h, pl.ds(start, SB_KB), :], qt_ref[0, h]) for h in range(N_HEADS)]
        log_beta, log_keep, pieces = [], [], []
        for h in range(N_HEADS):
            z = logits[h]
            ls = _log_sigmoid(z)
            lk = ls - z
            if strict is not None:
                lk = jnp.where(strict, lk, 0.0)
            hi = lk.astype(BF16)
            rem = lk - hi.astype(F32)
            mid = rem.astype(BF16)
            lo = (rem - mid.astype(F32)).astype(BF16)
            log_beta.append(ls)
            log_keep.append(lk)
            pieces.append((hi, mid, lo))
        between = [_dot(later, pieces[h][0]) + _dot(later, pieces[h][1]) + _dot(later, pieces[h][2])
                   for h in range(N_HEADS)]
        weights = []
        for h in range(N_HEADS):
            a = jnp.exp(log_beta[h] + between[h] + run_scr[h])
            if strict is not None:
                a = jnp.where(strict, a, 0.0)
            weights.append(a.astype(BF16))
        for h in range(N_HEADS):
            acc_scr[h] = acc_scr[h] + _dot(vt_ref[0, h, :, pl.ds(start, SB_KB)], weights[h])
            run_scr[h] = run_scr[h] + jnp.sum(log_keep[h], axis=0, keepdims=True)

    block(q0, lax.broadcasted_iota(I32, (SB_KB, tq), 0) < lax.broadcasted_iota(I32, (SB_KB, tq), 1))

    def cond(carry):
        return (carry[0] >= 0) & (carry[1] > SB_DEAD)

    def body(carry):
        block(pl.multiple_of(carry[0], SB_KB), None)
        return carry[0] - SB_KB, jnp.max(run_scr[...])

    lax.while_loop(cond, body, (q0 - SB_KB, jnp.max(run_scr[...])))
    for h in range(N_HEADS):
        o_ref[0, h * HEAD_DIM:(h + 1) * HEAD_DIM, :] = acc_scr[h]


def _sb_attn(qt, k, vt, later, past, tq):
    b, _, _, t = qt.shape
    l_sb = k.shape[2]
    return pl.pallas_call(
        functools.partial(_sb_kernel, past=past, tq=tq),
        grid=(b, t // tq),
        in_specs=[pl.BlockSpec((1, N_HEADS, HEAD_DIM, tq), lambda b, i: (b, 0, 0, i)),
                  _resident((1, N_HEADS, l_sb, HEAD_DIM), lambda b, i: (b, 0, 0, 0)),
                  _resident((1, N_HEADS, HEAD_DIM, l_sb), lambda b, i: (b, 0, 0, 0)),
                  _full(later.shape)],
        out_specs=pl.BlockSpec((1, GROUP_WIDTH, tq), lambda b, i: (b, 0, i)),
        out_shape=jax.ShapeDtypeStruct((b, GROUP_WIDTH, t), F32),
        scratch_shapes=[pltpu.VMEM((N_HEADS, 1, tq), F32), pltpu.VMEM((N_HEADS, HEAD_DIM, tq), F32)],
        compiler_params=_cparams(("parallel", "parallel")),
        name="stick_breaking",
    )(qt, k, vt, later)


def _n_far_blocks(q0):
    return jnp.maximum(q0 - FRONT + KB - 1, 0) // KB


def _far_start(q0, j):
    return pl.multiple_of(q0 - KB * (j + 1), FRONT)


def _flash_tile(n_maps, head_of, qa_ref, ka_ref, va_ref, state, q0, near_add, far_add):
    m_scr, acc_scr, s_scr, smax_scr = state
    m_scr[...] = jnp.full(m_scr.shape, NEG, F32)
    acc_scr[...] = jnp.zeros(acc_scr.shape, F32)

    def start_of(i):
        return pl.multiple_of(q0 - KB * i, FRONT)

    def scores(mp, start, add):
        s = _dot(ka_ref[0, head_of(mp), pl.ds(start, KB), :], qa_ref[0, mp])
        if add is not None:
            s = s + add(mp)
        s_scr[mp] = s
        smax_scr[mp] = jnp.max(s, axis=0, keepdims=True)

    def absorb(mp, start):
        m_old = m_scr[mp]
        m_new = jnp.maximum(m_old, smax_scr[mp])
        p = jnp.exp2(s_scr[mp] - m_new).astype(BF16)
        acc_scr[mp] = (acc_scr[mp] * jnp.exp2(m_old - m_new)
                       + _dot(va_ref[0, head_of(mp), :, pl.ds(start, KB)], p))
        m_scr[mp] = m_new

    for mp in range(n_maps):
        scores(mp, start_of(0), near_add)

    def body(i, c):
        nxt = start_of(i + 1)
        for mp in range(n_maps):
            absorb(mp, start_of(i))
            scores(mp, nxt, None if far_add is None else (lambda mp_: far_add(nxt)))
        return c

    n_far = _n_far_blocks(q0)
    lax.fori_loop(0, n_far, body, 0)
    for mp in range(n_maps):
        absorb(mp, start_of(n_far))
    return acc_scr


def _normalized(acc_scr, mp):
    return acc_scr[mp, 0:HEAD_DIM, :] / acc_scr[mp, HEAD_DIM:HEAD_DIM + 1, :]


def _flash_specs(n_maps, tq, lp, single_buffer=False):
    qa_spec = pl.BlockSpec((1, n_maps, LANES, tq), lambda b, i: (b, 0, 0, i))
    per_batch = _resident if single_buffer else pl.BlockSpec
    ka_spec = per_batch((1, N_HEADS, lp, LANES), lambda b, i: (b, 0, 0, 0))
    va_spec = per_batch((1, N_HEADS, V_ROWS, lp), lambda b, i: (b, 0, 0, 0))
    o_spec = pl.BlockSpec((1, GROUP_WIDTH, tq), lambda b, i: (b, 0, i))
    state = [pltpu.VMEM((n_maps, 1, tq), F32), pltpu.VMEM((n_maps, V_ROWS, tq), F32),
             pltpu.VMEM((n_maps, KB, tq), F32), pltpu.VMEM((n_maps, 1, tq), F32)]
    return qa_spec, ka_spec, va_spec, o_spec, state


def _fx_kernel(qa_ref, ka_ref, va_ref, causal_ref, o_ref, *state, past, tq):
    q0 = past + pl.program_id(1) * tq
    acc_scr = _flash_tile(N_HEADS, lambda mp: mp, qa_ref, ka_ref, va_ref, state, q0,
                          lambda mp: causal_ref[...], None)
    for h in range(N_HEADS):
        o_ref[0, h * HEAD_DIM:(h + 1) * HEAD_DIM, :] = _normalized(acc_scr, h)


def _fx_attn(qa, ka, va, causal, past, tq):
    b, _, _, t = qa.shape
    qa_spec, ka_spec, va_spec, o_spec, scratch = _flash_specs(N_HEADS, tq, ka.shape[2])
    return pl.pallas_call(
        functools.partial(_fx_kernel, past=past, tq=tq),
        grid=(b, t // tq),
        in_specs=[qa_spec, ka_spec, va_spec, _full(causal.shape)],
        out_specs=o_spec,
        out_shape=jax.ShapeDtypeStruct((b, GROUP_WIDTH, t), F32),
        scratch_shapes=scratch,
        compiler_params=_cparams(("parallel", "parallel")),
        name="forgetting",
    )(qa, ka, va, causal)


def _df_kernel(qa_ref, ka_ref, va_ref, near_ref, lam_ref, g_ref, o_ref, *state, past, tq):
    q0 = past + pl.program_id(1) * tq
    acc_scr = _flash_tile(2 * N_HEADS, lambda mp: mp // 2, qa_ref, ka_ref, va_ref, state, q0,
                          lambda mp: near_ref[mp // 2], None)
    lam_init = lam_ref[4:5, 0:1]
    lam = (jnp.exp(jnp.sum(lam_ref[0:1, :] * lam_ref[1:2, :], axis=-1, keepdims=True))
           - jnp.exp(jnp.sum(lam_ref[2:3, :] * lam_ref[3:4, :], axis=-1, keepdims=True)) + lam_init)
    for h in range(N_HEADS):
        o = _normalized(acc_scr, 2 * h) - lam * _normalized(acc_scr, 2 * h + 1)
        o = o * lax.rsqrt(jnp.mean(o * o, axis=0, keepdims=True) + SUBLN_EPS) * g_ref[...]
        o_ref[0, h * HEAD_DIM:(h + 1) * HEAD_DIM, :] = o * (1.0 - lam_init)


def _df_attn(qa, ka, va, near, lam_rows, g_cols, past, tq):
    b, _, _, t = qa.shape
    qa_spec, ka_spec, va_spec, o_spec, scratch = _flash_specs(2 * N_HEADS, tq, ka.shape[2])
    return pl.pallas_call(
        functools.partial(_df_kernel, past=past, tq=tq),
        grid=(b, t // tq),
        in_specs=[qa_spec, ka_spec, va_spec, _full(near.shape), _full(lam_rows.shape), _full(g_cols.shape)],
        out_specs=o_spec,
        out_shape=jax.ShapeDtypeStruct((b, GROUP_WIDTH, t), F32),
        scratch_shapes=scratch,
        compiler_params=_cparams(("parallel", "parallel")),
        name="differential",
    )(qa, ka, va, near, lam_rows, g_cols)


def _ds_kernel(qix_ref, wix_ref, kix_ref, qa_ref, ka_ref, va_ref, near_ref, upto_ref, o_ref,
               key_scr, k16_scr, msk_scr, *state, past, tq, l_keys, topk):
    q0 = past + pl.program_id(1) * tq
    near0 = pl.multiple_of(q0, FRONT)
    n_far = _n_far_blocks(q0)
    wix = wix_ref[0] * (N_IDX_HEADS ** -0.5)

    def score_block(start, near):
        kix = kix_ref[0, pl.ds(start, KB), :]
        sc = jnp.zeros((KB, tq), F32)
        for hh in range(N_IDX_HEADS):
            sc = sc + wix[hh:hh + 1, :] * jnp.maximum(_dot(kix, qix_ref[0, hh]), 0.0)
        kidx = start + lax.broadcasted_iota(I32, (KB, tq), 0)
        ok = (kidx >= FRONT) & (kidx < FRONT + l_keys)
        if near:
            ok = ok & (near_ref[0] > 0.5 * NEG)
        bits = lax.bitcast_convert_type(jnp.where(ok, sc, -jnp.inf), I32)
        key = bits ^ ((bits >> 31) & 0x7FFFFFFF)
        key_scr[pl.ds(start, KB), :] = key
        k16_scr[pl.ds(start, KB), :] = (key >> HALF_BITS).astype(I16)

    def each_block(fn):
        fn(near0)

        def far(j, c):
            fn(_far_start(q0, j))
            return c

        lax.fori_loop(0, n_far, far, 0)

    score_block(near0, True)

    def score_far(j, c):
        score_block(_far_start(q0, j), False)
        return c

    lax.fori_loop(0, n_far, score_far, 0)

    def count16(cand):
        cand16 = cand.astype(I16)

        def blk(start):
            ones = jnp.where(k16_scr[pl.ds(start, KB), :] >= cand16, jnp.ones((), BF16), jnp.zeros((), BF16))
            parts = [ones[r:r + PACKED_ROWS] for r in range(0, KB, PACKED_ROWS)]
            while len(parts) > 1:
                parts = [parts[r] + parts[r + 1] for r in range(0, len(parts), 2)]
            return parts[0].astype(F32)

        acc = lax.fori_loop(0, n_far, lambda j, c: c + blk(_far_start(q0, j)), blk(near0))
        return jnp.sum(acc, axis=0, keepdims=True)

    def greedy16(n_all):
        def bit_body(i, carry):
            u, n_u = carry
            cand = u | lax.shift_left(jnp.int32(1), HALF_BITS - 1 - i)
            n_cand = count16(cand - HALF_BIAS)
            take = n_cand >= topk
            return jnp.where(take, cand, u), jnp.where(take, n_cand, n_u)
        return lax.fori_loop(0, HALF_BITS, bit_body, (jnp.zeros((1, tq), I32), n_all))

    n_scratch = ((n_far + 1) * KB).astype(F32)
    hi_u, n_hi = greedy16(jnp.full((1, tq), n_scratch, F32))
    thr_hi = hi_u - HALF_BIAS

    def refine_block(start):
        key = key_scr[pl.ds(start, KB), :]
        hi = key >> HALF_BITS
        low = jnp.where(hi > thr_hi, HALF_BIAS - 1,
                        jnp.where(hi < thr_hi, -HALF_BIAS, (key & (2 * HALF_BIAS - 1)) - HALF_BIAS))
        k16_scr[pl.ds(start, KB), :] = low.astype(I16)

    each_block(refine_block)
    lo_u, n_ge = greedy16(n_hi)
    thr = thr_hi * (2 * HALF_BIAS) + lo_u

    tied = jnp.max(n_ge) > topk

    @pl.when(jnp.logical_not(tied))
    def _():
        def plain_block(start):
            key = key_scr[pl.ds(start, KB), :]
            msk_scr[pl.ds(start, KB), :] = jnp.where((key >= thr) & (key > KEY_NEG_INF), 0.0, NEG)
        each_block(plain_block)

    @pl.when(tied)
    def _():
        def above(start):
            return jnp.sum((key_scr[pl.ds(start, KB), :] > thr).astype(I32), axis=0, keepdims=True)

        n_gt = lax.fori_loop(0, n_far, lambda j, c: c + above(_far_start(q0, j)), above(near0))
        need = (topk - n_gt).astype(F32)

        def mask_block(start, seen):
            key = key_scr[pl.ds(start, KB), :]
            tie = key == thr
            rank = _dot(upto_ref[...], tie.astype(BF16)) + seen
            sel = (key > KEY_NEG_INF) & ((key > thr) | (tie & (rank <= need)))
            msk_scr[pl.ds(start, KB), :] = jnp.where(sel, 0.0, NEG)
            return seen + jnp.sum(tie.astype(F32), axis=0, keepdims=True)

        seen = lax.fori_loop(0, n_far, lambda i, c: mask_block(_far_start(q0, n_far - 1 - i), c),
                             jnp.zeros((1, tq), F32))
        mask_block(near0, seen)

    acc_scr = _flash_tile(N_HEADS, lambda mp: mp, qa_ref, ka_ref, va_ref, state, q0,
                          lambda mp: near_ref[mp] + msk_scr[pl.ds(near0, KB), :],
                          lambda start: msk_scr[pl.ds(start, KB), :])
    for h in range(N_HEADS):
        o_ref[0, h * HEAD_DIM:(h + 1) * HEAD_DIM, :] = _normalized(acc_scr, h)


def _ds_attn(qix, wix, kix, qa, ka, va, near, upto, past, tq, l_keys, topk):
    b, _, _, t = qa.shape
    lp = ka.shape[2]
    qa_spec, ka_spec, va_spec, o_spec, scratch = _flash_specs(N_HEADS, tq, lp, single_buffer=True)
    return pl.pallas_call(
        functools.partial(_ds_kernel, past=past, tq=tq, l_keys=l_keys, topk=topk),
        grid=(b, t // tq),
        in_specs=[pl.BlockSpec((1, N_IDX_HEADS, IDX_DIM, tq), lambda b, i: (b, 0, 0, i)),
                  pl.BlockSpec((1, N_IDX_HEADS, tq), lambda b, i: (b, 0, i)),
                  _resident((1, lp, IDX_DIM), lambda b, i: (b, 0, 0)),
                  qa_spec, ka_spec, va_spec, _full(near.shape), _full(upto.shape)],
        out_specs=o_spec,
        out_shape=jax.ShapeDtypeStruct((b, GROUP_WIDTH, t), F32),
        scratch_shapes=[pltpu.VMEM((lp, tq), I32), pltpu.VMEM((lp, tq), I16), pltpu.VMEM((lp, tq), F32)] + scratch,
        compiler_params=_cparams(("parallel", "parallel")),
        name="indexed_sparse",
    )(qix, wix, kix, qa, ka, va, near, upto)


def _post_kernel(x_ref, o0_ref, o1_ref, o2_ref, o3_ref, wo_ref, g_ref, wg_ref, wu_ref, wd_ref,
                 cw_ref, cb_ref, pc_ref, xo_ref, st_ref, uext, *, tm):
    t = pl.program_id(1)

    @pl.when(t == 0)
    def _():
        uext[0:SUBLANES, :] = pc_ref[0]

    @pl.when(t > 0)
    def _():
        uext[0:SUBLANES, :] = uext[tm:tm + SUBLANES, :]

    x = x_ref[0]
    for gi, ot_ref in enumerate((o0_ref, o1_ref, o2_ref, o3_ref)):
        x = x + lax.dot_general(ot_ref[0].astype(BF16), wo_ref[gi * GROUP_WIDTH:(gi + 1) * GROUP_WIDTH, :],
                                (((0,), (0,)), ((), ())), preferred_element_type=F32)
    hn = (x * lax.rsqrt(jnp.mean(x * x, axis=-1, keepdims=True) + RMS_EPS) * g_ref[...]).astype(BF16)
    u = _dot(hn, wg_ref[...])
    uext[SUBLANES:SUBLANES + tm, :] = u
    conv = cb_ref[...]
    conv = conv + cw_ref[0:1, :] * uext[SUBLANES - 2:SUBLANES - 2 + tm, :]
    conv = conv + cw_ref[1:2, :] * uext[SUBLANES - 1:SUBLANES - 1 + tm, :]
    conv = conv + cw_ref[2:3, :] * u
    hidden = conv * (1.0 / (1.0 + jnp.exp(-conv))) * _dot(hn, wu_ref[...])
    xo_ref[0] = x + _dot(hidden.astype(BF16), wd_ref[...])
    st_ref[0] = uext[tm:tm + SUBLANES, :]


def _post(x, outs, wo, g, wg, wu, wd, cw, cb, pconv, tm):
    b, t, d = x.shape
    f = wg.shape[1]
    row = lambda w: pl.BlockSpec((1, tm, w), lambda bi, ti: (bi, ti, 0))
    const = lambda shape: _resident(shape, lambda bi, ti: (0,) * len(shape))
    return pl.pallas_call(
        functools.partial(_post_kernel, tm=tm),
        grid=(b, t // tm),
        in_specs=[row(d)] + [pl.BlockSpec((1, GROUP_WIDTH, tm), lambda bi, ti: (bi, 0, ti))] * 4 + [
            const(wo.shape), const(g.shape), const(wg.shape), const(wu.shape), const(wd.shape),
            const(cw.shape), const(cb.shape),
            pl.BlockSpec((1, SUBLANES, f), lambda bi, ti: (bi, 0, 0))],
        out_specs=[row(d), pl.BlockSpec((1, SUBLANES, f), lambda bi, ti: (bi, 0, 0))],
        out_shape=[jax.ShapeDtypeStruct((b, t, d), F32),
                   jax.ShapeDtypeStruct((b, SUBLANES, f), F32)],
        scratch_shapes=[pltpu.VMEM((tm + SUBLANES, f), F32)],
        compiler_params=_cparams(("parallel", "arbitrary")),
        name="out_proj_mlp",
    )(x, *outs, wo, g, wg, wu, wd, cw, cb, pconv)


def _final_norm_kernel(x_ref, g_ref, o_ref):
    x = x_ref[...]
    o_ref[...] = x * lax.rsqrt(jnp.mean(x * x, axis=-1, keepdims=True) + RMS_EPS) * g_ref[...]


def _final_norm(x, g, tm):
    b, t, d = x.shape
    x2 = x.reshape(b * t, d)
    y = pl.pallas_call(
        _final_norm_kernel,
        grid=(b * t // tm,),
        in_specs=[pl.BlockSpec((tm, d), lambda i: (i, 0)), pl.BlockSpec((1, d), lambda i: (0, 0))],
        out_specs=pl.BlockSpec((tm, d), lambda i: (i, 0)),
        out_shape=jax.ShapeDtypeStruct(x2.shape, F32),
        compiler_params=_cparams(("parallel",)),
        name="final_norm",
    )(x2, g.reshape(1, d))
    return y.reshape(b, t, d)


def _t5_bucket(rel):
    half = T5_BUCKETS // 2
    max_exact = half // 2
    base = jnp.where(rel > 0, half, 0)
    n = jnp.abs(rel)
    nf = jnp.maximum(n, 1).astype(F32)
    large = max_exact + (jnp.log(nf / max_exact) / math.log(T5_MAX_DIST / max_exact)
                         * (half - max_exact)).astype(I32)
    large = jnp.minimum(large, half - 1)
    return base + jnp.where(n < max_exact, n, large)


def _bf16_part(x):
    bits = lax.bitcast_convert_type(x, jnp.uint32) & jnp.uint32(0xFFFF0000)
    return lax.bitcast_convert_type(bits, F32)


def _split3(x):
    hi = _bf16_part(x)
    mid = _bf16_part(x - hi)
    lo = _bf16_part(x - hi - mid)
    return jnp.stack([hi, mid, lo], axis=-1)


def _near_tables(t5_table, tq):
    kk = jnp.arange(KB, dtype=I32)[:, None] - FRONT
    qq = jnp.arange(tq, dtype=I32)[None, :]
    tab = t5_table.astype(F32).T
    far = tab[:, _t5_bucket(jnp.int32(-T5_MAX_DIST))]
    one_hot = (_t5_bucket(kk - qq)[:, :, None] == jnp.arange(T5_BUCKETS, dtype=I32)).astype(F32)
    bias = jnp.einsum("kqb,hb->hkq", one_hot, tab, precision=lax.Precision.HIGHEST)
    chunk_ok = (kk // CHUNK) <= (qq // CHUNK)
    near = jnp.where(chunk_ok[None], (bias - far[:, None, None]) * LOG2E, NEG)
    causal = jnp.where(kk <= qq, 0.0, NEG).astype(F32)
    return near, far * LOG2E, causal


def _pack_w_in(w_in):
    d = w_in.shape[0]
    fx0 = COL_SLAB + IDX_DIM + N_IDX_HEADS
    pad = jnp.zeros((d, LANES - SLAB_GATE - N_HEADS), w_in.dtype)
    return jnp.concatenate([w_in[:, :fx0], w_in[:, fx0 + 3 * GROUP_WIDTH:], pad,
                            w_in[:, fx0:fx0 + 3 * GROUP_WIDTH]], axis=1).astype(BF16)


def _with_past(past, new, l_pad):
    b, t = new.shape[:2]
    new = new.reshape(b, t, -1)
    if past is not None:
        new = jnp.concatenate([past.reshape(b, past.shape[1], -1).astype(F32), new], axis=1)
    return jnp.pad(new, ((0, 0), (0, l_pad - new.shape[1]), (0, 0)))


def _aug_queries(q, rows):
    qt = jnp.pad(jnp.transpose(q, (0, 2, 3, 1)), ((0, 0), (0, 0), (0, LANES - HEAD_DIM), (0, 0)))
    row = jnp.arange(LANES)[None, None, :, None]
    for e in range(N_EXTRA):
        qt = jnp.where(row == HEAD_DIM + e, rows[None, :, e, None, None], qt)
    return jnp.where(row == COL_FLAG, 1.0, qt).astype(BF16)


def _aug_keys(k, cols, lp):
    l = k.shape[1]
    lane = jnp.arange(LANES)
    body = jnp.pad(k, ((0, 0), (0, 0), (0, 0), (0, LANES - HEAD_DIM)))
    for e in range(N_EXTRA):
        body = jnp.where(lane == HEAD_DIM + e, cols[..., e:e + 1], body)
    body = jnp.pad(body, ((0, 0), (FRONT, lp - FRONT - l), (0, 0), (0, 0)))
    row = jnp.arange(lp)[None, :, None, None]
    is_pad = (row < FRONT) | (row >= FRONT + l)
    full = jnp.where(is_pad & (lane == COL_FLAG), NEG, body)
    return jnp.transpose(full, (0, 2, 1, 3)).astype(BF16)


def _aug_values(v, lp):
    l = v.shape[1]
    vt = jnp.pad(jnp.transpose(v, (0, 2, 3, 1)),
                 ((0, 0), (0, 0), (0, V_ROWS - HEAD_DIM), (FRONT, lp - FRONT - l)))
    row = jnp.arange(V_ROWS)[None, None, :, None]
    return jnp.where(row == HEAD_DIM, 1.0, vt).astype(BF16)


def _layer(x, past, lw, lam_init, t5_table, tm_proj, tm_post):
    (g_mix, w_in_p, b_forget, w_out, lam_q1, lam_k1, lam_q2, lam_k2, diff_subln,
     g_ffn, w_gate, w_up, w_down, conv_w, conv_b) = lw
    b, t, d = x.shape
    p_len = 0 if past is None else past[0].shape[1]
    l_keys = p_len + t
    l_pad = -(-l_keys // LANES) * LANES
    topk = min(TOPK_MAX, l_keys // 4)
    tq = min(t, TQ_MAX)
    assert p_len % FRONT == 0 and (t % TQ_MAX == 0 or t <= LANES) and FRONT == SB_KB == TQ_MAX
    lp = -(-max(FRONT + l_keys, p_len + t - tq + KB) // LANES) * LANES
    l_sb = p_len + t - tq + SB_KB
    (pk_sb, pv_sb, pk_df, pv_df, pk_ds, pv_ds, pk_ix, pk_fx, pv_fx, plogf, pconv) = (
        past if past is not None else (None,) * 11)

    bf = jnp.zeros((1, LANES), F32).at[0, SLAB_GATE:SLAB_GATE + N_HEADS].set(b_forget)
    y, lf = _proj(x.reshape(b * t, d), g_mix.reshape(1, d), w_in_p, bf, tm_proj)
    y = y.reshape(b, t, PROJ_PAD)
    grp = lambda i: y[:, :, i * GROUP_WIDTH:(i + 1) * GROUP_WIDTH]
    q_sb, k_sb, v_sb, q_df, k_df, v_df, q_ds, k_ds, v_ds = [grp(i) for i in range(9)]
    q_ix = y[:, :, COL_QIX:COL_SLAB]
    slab = y[:, :, COL_SLAB:COL_FX]
    k_ix = slab[:, :, :IDX_DIM]
    w_ix = slab[:, :, SLAB_WIX:SLAB_WIX + N_IDX_HEADS]
    q_fx, k_fx, v_fx = [y[:, :, COL_FX + i * GROUP_WIDTH:COL_FX + (i + 1) * GROUP_WIDTH] for i in range(3)]
    logf = lf.reshape(b, t, LANES)[:, :, SLAB_GATE:SLAB_GATE + N_HEADS]

    scale = HEAD_DIM ** -0.5
    hd = lambda a: a.reshape(b, -1, N_HEADS, HEAD_DIM)
    all_keys = lambda pst, new: hd(_with_past(pst, new, l_keys))
    near, far, causal = _near_tables(t5_table, tq)
    ones_cols = jnp.ones((b, l_keys, N_HEADS, N_EXTRA), F32)

    k_sb_all = jnp.transpose(hd(_with_past(pk_sb, k_sb, l_sb)), (0, 2, 1, 3)).astype(BF16)
    vt_sb_all = jnp.transpose(hd(_with_past(pv_sb, v_sb, l_sb)), (0, 2, 3, 1)).astype(BF16)
    later = (jnp.arange(SB_KB)[None, :] > jnp.arange(SB_KB)[:, None]).astype(BF16)
    o_sb = _sb_attn(jnp.transpose(hd(q_sb * scale), (0, 2, 3, 1)).astype(BF16), k_sb_all, vt_sb_all, later,
                    p_len, tq)

    half = jnp.arange(HEAD_DIM) < DIFF_DH
    q_df2 = jnp.where(jnp.stack([half, ~half])[None, None, None],
                      q_df.reshape(b, t, N_HEADS, 1, HEAD_DIM) * (DIFF_DH ** -0.5 * LOG2E), 0.0)
    lam_rows = jnp.zeros((SUBLANES, LANES), F32)
    for r, vec in enumerate((lam_q1, lam_k1, lam_q2, lam_k2)):
        lam_rows = lam_rows.at[r, :DIFF_DH].set(vec.astype(F32))
    lam_rows = lam_rows.at[4, :].set(lam_init)
    g_cols = jnp.broadcast_to(diff_subln.astype(F32)[:, None], (HEAD_DIM, tq))
    o_df = _df_attn(_aug_queries(q_df2.reshape(b, t, 2 * N_HEADS, HEAD_DIM),
                                 jnp.repeat(_split3(far[:N_HEADS]), 2, axis=0)),
                    _aug_keys(all_keys(pk_df, k_df), ones_cols, lp), _aug_values(all_keys(pv_df, v_df), lp),
                    near[:N_HEADS], lam_rows, g_cols, p_len, tq)

    kix_all = jnp.pad(_with_past(pk_ix, k_ix, l_keys), ((0, 0), (FRONT, lp - FRONT - l_keys), (0, 0))).astype(BF16)
    qix_t = jnp.transpose(q_ix.reshape(b, t, N_IDX_HEADS, IDX_DIM) * (IDX_DIM ** -0.5), (0, 2, 3, 1)).astype(BF16)
    upto = (jnp.arange(KB)[None, :] <= jnp.arange(KB)[:, None]).astype(BF16)
    o_ds = _ds_attn(qix_t, jnp.transpose(w_ix, (0, 2, 1)), kix_all,
                    _aug_queries(hd(q_ds * (scale * LOG2E)), _split3(far[N_HEADS:])),
                    _aug_keys(all_keys(pk_ds, k_ds), ones_cols, lp), _aug_values(all_keys(pv_ds, v_ds), lp),
                    near[N_HEADS:], upto, p_len, tq, l_keys, topk)

    c_all = _cumsum_seq(_with_past(plogf, logf, l_pad))
    c_keys = jnp.transpose(c_all[:, :, :l_keys], (0, 2, 1))
    o_fx = _fx_attn(_aug_queries(hd(q_fx * (scale * LOG2E)), jnp.ones((N_HEADS, N_EXTRA), F32)),
                    _aug_keys(all_keys(pk_fx, k_fx), _split3(-LOG2E * c_keys), lp),
                    _aug_values(all_keys(pv_fx, v_fx), lp), causal, p_len, tq)

    f = w_gate.shape[1]
    if pconv is None:
        pc = jnp.zeros((b, SUBLANES, f), F32)
    else:
        pc = jnp.pad(pconv.astype(F32), ((0, 0), (SUBLANES - (CONV_W - 1), 0), (0, 0)))
    cw = jnp.pad(conv_w.astype(F32), ((0, SUBLANES - CONV_W), (0, 0)))
    x_new, st = _post(x, (o_sb, o_df, o_ds, o_fx), w_out, g_ffn.reshape(1, d), w_gate, w_up,
                      w_down, cw, conv_b.reshape(1, f).astype(F32), pc, tm_post)

    rows = (hd(k_sb), hd(v_sb), k_df.reshape(b, t, N_HEADS, 2, DIFF_DH), hd(v_df), hd(k_ds), hd(v_ds),
            k_ix, hd(k_fx), hd(v_fx), logf, st[:, SUBLANES - (CONV_W - 1):])
    return x_new, rows


def _row_tile(n, cap):
    tm = min(n, cap)
    while n % tm:
        tm -= SUBLANES
    assert tm > 0
    return tm


def kernel(x_prompt, x_sample, cache_sb_k, cache_sb_v, cache_diff_k, cache_diff_v, cache_dsa_k, cache_dsa_v, cache_dsa_kidx, cache_fox_k, cache_fox_v, cache_fox_logf, state_ffn_conv, g_mix, w_in, b_forget, w_out, lam_q1, lam_k1, lam_q2, lam_k2, diff_subln, g_ffn, w_gate, w_up, w_down, conv_w, conv_b, t5_table, g_final):
    depth = w_in.shape[0]
    bp, tp, _ = x_prompt.shape
    bs, ts, _ = x_sample.shape
    xp, xs = x_prompt, x_sample
    p_rows, s_rows = [], []
    for l in range(depth):
        lw = (g_mix[l], _pack_w_in(w_in[l]), b_forget[l], w_out[l].astype(BF16), lam_q1[l], lam_k1[l],
              lam_q2[l], lam_k2[l], diff_subln[l], g_ffn[l], w_gate[l].astype(BF16), w_up[l].astype(BF16),
              w_down[l].astype(BF16), conv_w[l], conv_b[l])
        lam_init = 0.8 - 0.6 * math.exp(-0.3 * l)
        xp, rp = _layer(xp, None, lw, lam_init, t5_table, _row_tile(bp * tp, 1024), _row_tile(tp, 512))
        s_past = (cache_sb_k[l], cache_sb_v[l], cache_diff_k[l], cache_diff_v[l], cache_dsa_k[l],
                  cache_dsa_v[l], cache_dsa_kidx[l], cache_fox_k[l], cache_fox_v[l], cache_fox_logf[l],
                  state_ffn_conv[l])
        xs, rs = _layer(xs, s_past, lw, lam_init, t5_table, _row_tile(bs * ts, 512), _row_tile(ts, 256))
        p_rows.append(rp)
        s_rows.append(rs)
    y_prompt = _final_norm(xp, g_final, _row_tile(bp * tp, 512))
    y_sample = _final_norm(xs, g_final, _row_tile(bs * ts, 512))
    stack = lambda rows, i: jnp.stack([r[i] for r in rows], axis=0)
    return ((y_prompt, y_sample) + tuple(stack(p_rows, i) for i in range(11))
            + tuple(stack(s_rows, i) for i in range(11)))
```

```python
import functools
import math

import jax
import jax.numpy as jnp
from jax import lax
from jax.experimental import pallas as pl
from jax.experimental.pallas import tpu as pltpu

F32 = jnp.float32
BF16 = jnp.bfloat16
I32 = jnp.int32
I16 = jnp.int16

N_HEADS = 4
HEAD_DIM = 64
GROUP_WIDTH = N_HEADS * HEAD_DIM
DIFF_DH = HEAD_DIM // 2
N_IDX_HEADS = 8
IDX_DIM = 64
CHUNK = 64
TOPK_MAX = 256
T5_BUCKETS = 32
T5_MAX_DIST = 128
CONV_W = 3
RMS_EPS = 1e-6
SUBLN_EPS = 1e-5

LANES = 128
SUBLANES = 8
SB_KB = 256
KB = 512
FRONT = 256
TQ_MAX = 256
N_EXTRA = 3
COL_FLAG = HEAD_DIM + N_EXTRA
V_ROWS = 80
NEG = -1e30
SB_DEAD = -110.0
HALF_BITS = 16
HALF_BIAS = 2 ** (HALF_BITS - 1)
PACKED_ROWS = 16
KEY_NEG_INF = -2139095041
LOG2E = math.log2(math.e)
VMEM_LIMIT = 56 * 1024 * 1024
SMALL_BLOCK_ELEMS = 1024 * 1024

COL_QIX = 9 * GROUP_WIDTH
COL_SLAB = COL_QIX + N_IDX_HEADS * IDX_DIM
SLAB_WIX = IDX_DIM
SLAB_GATE = IDX_DIM + N_IDX_HEADS
COL_FX = COL_SLAB + LANES
PROJ_PAD = COL_FX + 3 * GROUP_WIDTH


def _cparams(sem):
    return pltpu.CompilerParams(dimension_semantics=sem, vmem_limit_bytes=VMEM_LIMIT)


def _resident(shape, index_map):
    if math.prod(shape) <= SMALL_BLOCK_ELEMS:
        return pl.BlockSpec(shape, index_map)
    return pl.BlockSpec(shape, index_map, pipeline_mode=pl.Buffered(1))


def _full(shape):
    return _resident(shape, lambda b, i: (0,) * len(shape))


def _log_sigmoid(z):
    return jnp.minimum(z, 0.0) - jnp.log(1.0 + jnp.exp(-jnp.abs(z)))


def _dot(a, b):
    return jnp.dot(a, b, preferred_element_type=F32)


def _proj_kernel(x_ref, g_ref, w_ref, bf_ref, y_ref, lf_ref):
    x = x_ref[...]
    hn = x * lax.rsqrt(jnp.mean(x * x, axis=-1, keepdims=True) + RMS_EPS) * g_ref[...]
    y = _dot(hn.astype(BF16), w_ref[...])
    y_ref[...] = y
    lf_ref[...] = _log_sigmoid(y[:, COL_SLAB:COL_SLAB + LANES] + bf_ref[...])


def _proj(x2, g, w, bf, tm):
    n, d = x2.shape
    return pl.pallas_call(
        _proj_kernel,
        grid=(n // tm,),
        in_specs=[pl.BlockSpec((tm, d), lambda i: (i, 0)),
                  pl.BlockSpec((1, d), lambda i: (0, 0)),
                  _resident((d, PROJ_PAD), lambda i: (0, 0)),
                  pl.BlockSpec((1, LANES), lambda i: (0, 0))],
        out_specs=[pl.BlockSpec((tm, PROJ_PAD), lambda i: (i, 0)),
                   pl.BlockSpec((tm, LANES), lambda i: (i, 0))],
        out_shape=[jax.ShapeDtypeStruct((n, PROJ_PAD), F32),
                   jax.ShapeDtypeStruct((n, LANES), F32)],
        compiler_params=_cparams(("parallel",)),
        name="proj",
    )(x2, g, w, bf)


def _cumsum_kernel(x_ref, o_ref):
    nch, rows, _ = x_ref.shape
    lane = lax.broadcasted_iota(I32, (rows, LANES), 1)

    def body(c, carry):
        x = x_ref[c]
        d = 1
        while d < LANES:
            x = x + jnp.where(lane >= d, pltpu.roll(x, d, 1), 0.0)
            d *= 2
        x = x + carry
        o_ref[c] = x
        return jnp.broadcast_to(x[:, LANES - 1:LANES], (rows, LANES))

    lax.fori_loop(0, nch, body, jnp.zeros((rows, LANES), F32))


def _cumsum_seq(logf_all):
    b, l, h = logf_all.shape
    rows = b * h
    rows_pad = -(-rows // SUBLANES) * SUBLANES
    x = jnp.transpose(logf_all, (0, 2, 1)).reshape(rows, l // LANES, LANES)
    x = jnp.pad(jnp.transpose(x, (1, 0, 2)), ((0, 0), (0, rows_pad - rows), (0, 0)))
    c = pl.pallas_call(
        _cumsum_kernel,
        out_shape=jax.ShapeDtypeStruct(x.shape, F32),
        compiler_params=pltpu.CompilerParams(vmem_limit_bytes=VMEM_LIMIT),
        name="cumsum",
    )(x)
    return jnp.transpose(c[:, :rows], (1, 0, 2)).reshape(b, h, l)


def _sb_kernel(qt_ref, k_ref, vt_ref, later_ref, o_ref, run_scr, acc_scr, *, past, tq):
    q0 = pl.multiple_of(past + pl.program_id(1) * tq, SB_KB)
    run_scr[...] = jnp.zeros(run_scr.shape, F32)
    acc_scr[...] = jnp.zeros(acc_scr.shape, F32)
    later = later_ref[...]

    def block(start, strict):
        logits = [_dot(k_ref[0, h, pl.ds(start, SB_KB), :], qt_ref[0, h]) for h in range(N_HEADS)]
        log_beta, log_keep, pieces = [], [], []
        for h in range(N_HEADS):
            z = logits[h]
            ls = _log_sigmoid(z)
            lk = ls - z
            if strict is not None:
                lk = jnp.where(strict, lk, 0.0)
            hi = lk.astype(BF16)
            rem = lk - hi.astype(F32)
            mid = rem.astype(BF16)
            lo = (rem - mid.astype(F32)).astype(BF16)
            log_beta.append(ls)
            log_keep.append(lk)
            pieces.append((hi, mid, lo))
        between = [_dot(later, pieces[h][0]) + _dot(later, pieces[h][1]) + _dot(later, pieces[h][2])
                   for h in range(N_HEADS)]
        weights = []
        for h in range(N_HEADS):
            a = jnp.exp(log_beta[h] + between[h] + run_scr[h])
            if strict is not None:
                a = jnp.where(strict, a, 0.0)
            weights.append(a.astype(BF16))
        for h in range(N_HEADS):
            acc_scr[h] = acc_scr[h] + _dot(vt_ref[0, h, :, pl.ds(start, SB_KB)], weights[h])
            run_scr[h] = run_scr[h] + jnp.sum(log_keep[h], axis=0, keepdims=True)

    block(q0, lax.broadcasted_iota(I32, (SB_KB, tq), 0) < lax.broadcasted_iota(I32, (SB_KB, tq), 1))

    def cond(carry):
        return (carry[0] >= 0) & (carry[1] > SB_DEAD)

    def body(carry):
        block(pl.multiple_of(carry[0], SB_KB), None)
        return carry[0] - SB_KB, jnp.max(run_scr[...])

    lax.while_loop(cond, body, (q0 - SB_KB, jnp.max(run_scr[...])))
    for h in range(N_HEADS):
        o_ref[0, h * HEAD_DIM:(h + 1) * HEAD_DIM, :] = acc_scr[h]


def _sb_attn(qt, k, vt, later, past, tq):
    b, _, _, t = qt.shape
    l_sb = k.shape[2]
    return pl.pallas_call(
        functools.partial(_sb_kernel, past=past, tq=tq),
        grid=(b, t // tq),
        in_specs=[pl.BlockSpec((1, N_HEADS, HEAD_DIM, tq), lambda b, i: (b, 0, 0, i)),
                  _resident((1, N_HEADS, l_sb, HEAD_DIM), lambda b, i: (b, 0, 0, 0)),
                  _resident((1, N_HEADS, HEAD_DIM, l_sb), lambda b, i: (b, 0, 0, 0)),
                  _full(later.shape)],
        out_specs=pl.BlockSpec((1, GROUP_WIDTH, tq), lambda b, i: (b, 0, i)),
        out_shape=jax.ShapeDtypeStruct((b, GROUP_WIDTH, t), F32),
        scratch_shapes=[pltpu.VMEM((N_HEADS, 1, tq), F32), pltpu.VMEM((N_HEADS, HEAD_DIM, tq), F32)],
        compiler_params=_cparams(("parallel", "parallel")),
        name="stick_breaking",
    )(qt, k, vt, later)


def _n_far_blocks(q0):
    return jnp.maximum(q0 - FRONT + KB - 1, 0) // KB


def _far_start(q0, j):
    return pl.multiple_of(q0 - KB * (j + 1), FRONT)


def _flash_tile(n_maps, head_of, qa_ref, ka_ref, va_ref, state, q0, near_add, far_add):
    m_scr, acc_scr, s_scr, smax_scr = state
    m_scr[...] = jnp.full(m_scr.shape, NEG, F32)
    acc_scr[...] = jnp.zeros(acc_scr.shape, F32)

    def start_of(i):
        return pl.multiple_of(q0 - KB * i, FRONT)

    def scores(mp, start, add):
        s = _dot(ka_ref[0, head_of(mp), pl.ds(start, KB), :], qa_ref[0, mp])
        if add is not None:
            s = s + add(mp)
        s_scr[mp] = s
        smax_scr[mp] = jnp.max(s, axis=0, keepdims=True)

    def absorb(mp, start):
        m_old = m_scr[mp]
        m_new = jnp.maximum(m_old, smax_scr[mp])
        p = jnp.exp2(s_scr[mp] - m_new).astype(BF16)
        acc_scr[mp] = (acc_scr[mp] * jnp.exp2(m_old - m_new)
                       + _dot(va_ref[0, head_of(mp), :, pl.ds(start, KB)], p))
        m_scr[mp] = m_new

    for mp in range(n_maps):
        scores(mp, start_of(0), near_add)

    def body(i, c):
        nxt = start_of(i + 1)
        for mp in range(n_maps):
            absorb(mp, start_of(i))
            scores(mp, nxt, None if far_add is None else (lambda mp_: far_add(nxt)))
        return c

    n_far = _n_far_blocks(q0)
    lax.fori_loop(0, n_far, body, 0)
    for mp in range(n_maps):
        absorb(mp, start_of(n_far))
    return acc_scr


def _normalized(acc_scr, mp):
    return acc_scr[mp, 0:HEAD_DIM, :] / acc_scr[mp, HEAD_DIM:HEAD_DIM + 1, :]


def _flash_specs(n_maps, tq, lp, single_buffer=False):
    qa_spec = pl.BlockSpec((1, n_maps, LANES, tq), lambda b, i: (b, 0, 0, i))
    per_batch = _resident if single_buffer else pl.BlockSpec
    ka_spec = per_batch((1, N_HEADS, lp, LANES), lambda b, i: (b, 0, 0, 0))
    va_spec = per_batch((1, N_HEADS, V_ROWS, lp), lambda b, i: (b, 0, 0, 0))
    o_spec = pl.BlockSpec((1, GROUP_WIDTH, tq), lambda b, i: (b, 0, i))
    state = [pltpu.VMEM((n_maps, 1, tq), F32), pltpu.VMEM((n_maps, V_ROWS, tq), F32),
             pltpu.VMEM((n_maps, KB, tq), F32), pltpu.VMEM((n_maps, 1, tq), F32)]
    return qa_spec, ka_spec, va_spec, o_spec, state


def _fx_kernel(qa_ref, ka_ref, va_ref, causal_ref, o_ref, *state, past, tq):
    q0 = past + pl.program_id(1) * tq
    acc_scr = _flash_tile(N_HEADS, lambda mp: mp, qa_ref, ka_ref, va_ref, state, q0,
                          lambda mp: causal_ref[...], None)
    for h in range(N_HEADS):
        o_ref[0, h * HEAD_DIM:(h + 1) * HEAD_DIM, :] = _normalized(acc_scr, h)


def _fx_attn(qa, ka, va, causal, past, tq):
    b, _, _, t = qa.shape
    qa_spec, ka_spec, va_spec, o_spec, scratch = _flash_specs(N_HEADS, tq, ka.shape[2])
    return pl.pallas_call(
        functools.partial(_fx_kernel, past=past, tq=tq),
        grid=(b, t // tq),
        in_specs=[qa_spec, ka_spec, va_spec, _full(causal.shape)],
        out_specs=o_spec,
        out_shape=jax.ShapeDtypeStruct((b, GROUP_WIDTH, t), F32),
        scratch_shapes=scratch,
        compiler_params=_cparams(("parallel", "parallel")),
        name="forgetting",
    )(qa, ka, va, causal)


def _df_kernel(qa_ref, ka_ref, va_ref, near_ref, lam_ref, g_ref, o_ref, *state, past, tq):
    q0 = past + pl.program_id(1) * tq
    acc_scr = _flash_tile(2 * N_HEADS, lambda mp: mp // 2, qa_ref, ka_ref, va_ref, state, q0,
                          lambda mp: near_ref[mp // 2], None)
    lam_init = lam_ref[4:5, 0:1]
    lam = (jnp.exp(jnp.sum(lam_ref[0:1, :] * lam_ref[1:2, :], axis=-1, keepdims=True))
           - jnp.exp(jnp.sum(lam_ref[2:3, :] * lam_ref[3:4, :], axis=-1, keepdims=True)) + lam_init)
    for h in range(N_HEADS):
        o = _normalized(acc_scr, 2 * h) - lam * _normalized(acc_scr, 2 * h + 1)
        o = o * lax.rsqrt(jnp.mean(o * o, axis=0, keepdims=True) + SUBLN_EPS) * g_ref[...]
        o_ref[0, h * HEAD_DIM:(h + 1) * HEAD_DIM, :] = o * (1.0 - lam_init)


def _df_attn(qa, ka, va, near, lam_rows, g_cols, past, tq):
    b, _, _, t = qa.shape
    qa_spec, ka_spec, va_spec, o_spec, scratch = _flash_specs(2 * N_HEADS, tq, ka.shape[2])
    return pl.pallas_call(
        functools.partial(_df_kernel, past=past, tq=tq),
        grid=(b, t // tq),
        in_specs=[qa_spec, ka_spec, va_spec, _full(near.shape), _full(lam_rows.shape), _full(g_cols.shape)],
        out_specs=o_spec,
        out_shape=jax.ShapeDtypeStruct((b, GROUP_WIDTH, t), F32),
        scratch_shapes=scratch,
        compiler_params=_cparams(("parallel", "parallel")),
        name="differential",
    )(qa, ka, va, near, lam_rows, g_cols)


def _ds_kernel(qix_ref, wix_ref, kix_ref, qa_ref, ka_ref, va_ref, near_ref, upto_ref, o_ref,
               key_scr, k16_scr, msk_scr, *state, past, tq, l_keys, topk):
    q0 = past + pl.program_id(1) * tq
    near0 = pl.multiple_of(q0, FRONT)
    n_far = _n_far_blocks(q0)
    wix = wix_ref[0] * (N_IDX_HEADS ** -0.5)

    def score_block(start, near):
        kix = kix_ref[0, pl.ds(start, KB), :]
        sc = jnp.zeros((KB, tq), F32)
        for hh in range(N_IDX_HEADS):
            sc = sc + wix[hh:hh + 1, :] * jnp.maximum(_dot(kix, qix_ref[0, hh]), 0.0)
        kidx = start + lax.broadcasted_iota(I32, (KB, tq), 0)
        ok = (kidx >= FRONT) & (kidx < FRONT + l_keys)
        if near:
            ok = ok & (near_ref[0] > 0.5 * NEG)
        bits = lax.bitcast_convert_type(jnp.where(ok, sc, -jnp.inf), I32)
        key = bits ^ ((bits >> 31) & 0x7FFFFFFF)
        key_scr[pl.ds(start, KB), :] = key
        k16_scr[pl.ds(start, KB), :] = (key >> HALF_BITS).astype(I16)

    def each_block(fn):
        fn(near0)

        def far(j, c):
            fn(_far_start(q0, j))
            return c

        lax.fori_loop(0, n_far, far, 0)

    score_block(near0, True)

    def score_far(j, c):
        score_block(_far_start(q0, j), False)
        return c

    lax.fori_loop(0, n_far, score_far, 0)

    def count16(cand):
        cand16 = cand.astype(I16)

        def blk(start):
            ones = jnp.where(k16_scr[pl.ds(start, KB), :] >= cand16, jnp.ones((), BF16), jnp.zeros((), BF16))
            parts = [ones[r:r + PACKED_ROWS] for r in range(0, KB, PACKED_ROWS)]
            while len(parts) > 1:
                parts = [parts[r] + parts[r + 1] for r in range(0, len(parts), 2)]
            return parts[0].astype(F32)

        acc = lax.fori_loop(0, n_far // 2,
                            lambda j, c: c + (blk(_far_start(q0, 2 * j)) + blk(_far_start(q0, 2 * j + 1))),
                            blk(near0))
        acc = acc + lax.cond(n_far % 2 == 1, lambda: blk(_far_start(q0, n_far - 1)),
                             lambda: jnp.zeros((PACKED_ROWS, tq), F32))
        return jnp.sum(acc, axis=0, keepdims=True)

    def greedy16(n_all):
        def bit_body(i, carry):
            u, n_u = carry
            cand = u | lax.shift_left(jnp.int32(1), HALF_BITS - 1 - i)
            n_cand = count16(cand - HALF_BIAS)
            take = n_cand >= topk
            return jnp.where(take, cand, u), jnp.where(take, n_cand, n_u)
        return lax.fori_loop(0, HALF_BITS, bit_body, (jnp.zeros((1, tq), I32), n_all))

    n_scratch = ((n_far + 1) * KB).astype(F32)
    hi_u, n_hi = greedy16(jnp.full((1, tq), n_scratch, F32))
    thr_hi = hi_u - HALF_BIAS

    def refine_block(start):
        key = key_scr[pl.ds(start, KB), :]
        hi = key >> HALF_BITS
        low = jnp.where(hi > thr_hi, HALF_BIAS - 1,
                        jnp.where(hi < thr_hi, -HALF_BIAS, (key & (2 * HALF_BIAS - 1)) - HALF_BIAS))
        k16_scr[pl.ds(start, KB), :] = low.astype(I16)

    each_block(refine_block)
    lo_u, n_ge = greedy16(n_hi)
    thr = thr_hi * (2 * HALF_BIAS) + lo_u

    tied = jnp.max(n_ge) > topk

    @pl.when(jnp.logical_not(tied))
    def _():
        def plain_block(start):
            key = key_scr[pl.ds(start, KB), :]
            msk_scr[pl.ds(start, KB), :] = jnp.where((key >= thr) & (key > KEY_NEG_INF), 0.0, NEG)
        each_block(plain_block)

    @pl.when(tied)
    def _():
        def above(start):
            return jnp.sum((key_scr[pl.ds(start, KB), :] > thr).astype(I32), axis=0, keepdims=True)

        n_gt = lax.fori_loop(0, n_far, lambda j, c: c + above(_far_start(q0, j)), above(near0))
        need = (topk - n_gt).astype(F32)

        def mask_block(start, seen):
            key = key_scr[pl.ds(start, KB), :]
            tie = key == thr
            rank = _dot(upto_ref[...], tie.astype(BF16)) + seen
            sel = (key > KEY_NEG_INF) & ((key > thr) | (tie & (rank <= need)))
            msk_scr[pl.ds(start, KB), :] = jnp.where(sel, 0.0, NEG)
            return seen + jnp.sum(tie.astype(F32), axis=0, keepdims=True)

        seen = lax.fori_loop(0, n_far, lambda i, c: mask_block(_far_start(q0, n_far - 1 - i), c),
                             jnp.zeros((1, tq), F32))
        mask_block(near0, seen)

    acc_scr = _flash_tile(N_HEADS, lambda mp: mp, qa_ref, ka_ref, va_ref, state, q0,
                          lambda mp: near_ref[mp] + msk_scr[pl.ds(near0, KB), :],
                          lambda start: msk_scr[pl.ds(start, KB), :])
    for h in range(N_HEADS):
        o_ref[0, h * HEAD_DIM:(h + 1) * HEAD_DIM, :] = _normalized(acc_scr, h)


def _ds_attn(qix, wix, kix, qa, ka, va, near, upto, past, tq, l_keys, topk):
    b, _, _, t = qa.shape
    lp = ka.shape[2]
    qa_spec, ka_spec, va_spec, o_spec, scratch = _flash_specs(N_HEADS, tq, lp, single_buffer=True)
    return pl.pallas_call(
        functools.partial(_ds_kernel, past=past, tq=tq, l_keys=l_keys, topk=topk),
        grid=(b, t // tq),
        in_specs=[pl.BlockSpec((1, N_IDX_HEADS, IDX_DIM, tq), lambda b, i: (b, 0, 0, i)),
                  pl.BlockSpec((1, N_IDX_HEADS, tq), lambda b, i: (b, 0, i)),
                  _resident((1, lp, IDX_DIM), lambda b, i: (b, 0, 0)),
                  qa_spec, ka_spec, va_spec, _full(near.shape), _full(upto.shape)],
        out_specs=o_spec,
        out_shape=jax.ShapeDtypeStruct((b, GROUP_WIDTH, t), F32),
        scratch_shapes=[pltpu.VMEM((lp, tq), I32), pltpu.VMEM((lp, tq), I16), pltpu.VMEM((lp, tq), F32)] + scratch,
        compiler_params=_cparams(("parallel", "parallel")),
        name="indexed_sparse",
    )(qix, wix, kix, qa, ka, va, near, upto)


def _post_kernel(x_ref, o0_ref, o1_ref, o2_ref, o3_ref, wo_ref, g_ref, wg_ref, wu_ref, wd_ref,
                 cw_ref, cb_ref, pc_ref, xo_ref, st_ref, uext, *, tm):
    t = pl.program_id(1)

    @pl.when(t == 0)
    def _():
        uext[0:SUBLANES, :] = pc_ref[0]

    @pl.when(t > 0)
    def _():
        uext[0:SUBLANES, :] = uext[tm:tm + SUBLANES, :]

    x = x_ref[0]
    for gi, ot_ref in enumerate((o0_ref, o1_ref, o2_ref, o3_ref)):
        x = x + lax.dot_general(ot_ref[0].astype(BF16), wo_ref[gi * GROUP_WIDTH:(gi + 1) * GROUP_WIDTH, :],
                                (((0,), (0,)), ((), ())), preferred_element_type=F32)
    hn = (x * lax.rsqrt(jnp.mean(x * x, axis=-1, keepdims=True) + RMS_EPS) * g_ref[...]).astype(BF16)
    u = _dot(hn, wg_ref[...])
    uext[SUBLANES:SUBLANES + tm, :] = u
    conv = cb_ref[...]
    conv = conv + cw_ref[0:1, :] * uext[SUBLANES - 2:SUBLANES - 2 + tm, :]
    conv = conv + cw_ref[1:2, :] * uext[SUBLANES - 1:SUBLANES - 1 + tm, :]
    conv = conv + cw_ref[2:3, :] * u
    hidden = conv * (1.0 / (1.0 + jnp.exp(-conv))) * _dot(hn, wu_ref[...])
    xo_ref[0] = x + _dot(hidden.astype(BF16), wd_ref[...])
    st_ref[0] = uext[tm:tm + SUBLANES, :]


def _post(x, outs, wo, g, wg, wu, wd, cw, cb, pconv, tm):
    b, t, d = x.shape
    f = wg.shape[1]
    row = lambda w: pl.BlockSpec((1, tm, w), lambda bi, ti: (bi, ti, 0))
    const = lambda shape: _resident(shape, lambda bi, ti: (0,) * len(shape))
    return pl.pallas_call(
        functools.partial(_post_kernel, tm=tm),
        grid=(b, t // tm),
        in_specs=[row(d)] + [pl.BlockSpec((1, GROUP_WIDTH, tm), lambda bi, ti: (bi, 0, ti))] * 4 + [
            const(wo.shape), const(g.shape), const(wg.shape), const(wu.shape), const(wd.shape),
            const(cw.shape), const(cb.shape),
            pl.BlockSpec((1, SUBLANES, f), lambda bi, ti: (bi, 0, 0))],
        out_specs=[row(d), pl.BlockSpec((1, SUBLANES, f), lambda bi, ti: (bi, 0, 0))],
        out_shape=[jax.ShapeDtypeStruct((b, t, d), F32),
                   jax.ShapeDtypeStruct((b, SUBLANES, f), F32)],
        scratch_shapes=[pltpu.VMEM((tm + SUBLANES, f), F32)],
        compiler_params=_cparams(("parallel", "arbitrary")),
        name="out_proj_mlp",
    )(x, *outs, wo, g, wg, wu, wd, cw, cb, pconv)


def _final_norm_kernel(x_ref, g_ref, o_ref):
    x = x_ref[...]
    o_ref[...] = x * lax.rsqrt(jnp.mean(x * x, axis=-1, keepdims=True) + RMS_EPS) * g_ref[...]


def _final_norm(x, g, tm):
    b, t, d = x.shape
    x2 = x.reshape(b * t, d)
    y = pl.pallas_call(
        _final_norm_kernel,
        grid=(b * t // tm,),
        in_specs=[pl.BlockSpec((tm, d), lambda i: (i, 0)), pl.BlockSpec((1, d), lambda i: (0, 0))],
        out_specs=pl.BlockSpec((tm, d), lambda i: (i, 0)),
        out_shape=jax.ShapeDtypeStruct(x2.shape, F32),
        compiler_params=_cparams(("parallel",)),
        name="final_norm",
    )(x2, g.reshape(1, d))
    return y.reshape(b, t, d)


def _t5_bucket(rel):
    half = T5_BUCKETS // 2
    max_exact = half // 2
    base = jnp.where(rel > 0, half, 0)
    n = jnp.abs(rel)
    nf = jnp.maximum(n, 1).astype(F32)
    large = max_exact + (jnp.log(nf / max_exact) / math.log(T5_MAX_DIST / max_exact)
                         * (half - max_exact)).astype(I32)
    large = jnp.minimum(large, half - 1)
    return base + jnp.where(n < max_exact, n, large)


def _bf16_part(x):
    bits = lax.bitcast_convert_type(x, jnp.uint32) & jnp.uint32(0xFFFF0000)
    return lax.bitcast_convert_type(bits, F32)


def _split3(x):
    hi = _bf16_part(x)
    mid = _bf16_part(x - hi)
    lo = _bf16_part(x - hi - mid)
    return jnp.stack([hi, mid, lo], axis=-1)


def _near_tables(t5_table, tq):
    kk = jnp.arange(KB, dtype=I32)[:, None] - FRONT
    qq = jnp.arange(tq, dtype=I32)[None, :]
    tab = t5_table.astype(F32).T
    far = tab[:, _t5_bucket(jnp.int32(-T5_MAX_DIST))]
    one_hot = (_t5_bucket(kk - qq)[:, :, None] == jnp.arange(T5_BUCKETS, dtype=I32)).astype(F32)
    bias = jnp.einsum("kqb,hb->hkq", one_hot, tab, precision=lax.Precision.HIGHEST)
    chunk_ok = (kk // CHUNK) <= (qq // CHUNK)
    near = jnp.where(chunk_ok[None], (bias - far[:, None, None]) * LOG2E, NEG)
    causal = jnp.where(kk <= qq, 0.0, NEG).astype(F32)
    return near, far * LOG2E, causal


def _pack_w_in(w_in):
    d = w_in.shape[0]
    fx0 = COL_SLAB + IDX_DIM + N_IDX_HEADS
    pad = jnp.zeros((d, LANES - SLAB_GATE - N_HEADS), w_in.dtype)
    return jnp.concatenate([w_in[:, :fx0], w_in[:, fx0 + 3 * GROUP_WIDTH:], pad,
                            w_in[:, fx0:fx0 + 3 * GROUP_WIDTH]], axis=1).astype(BF16)


def _with_past(past, new, l_pad):
    b, t = new.shape[:2]
    new = new.reshape(b, t, -1)
    if past is not None:
        new = jnp.concatenate([past.reshape(b, past.shape[1], -1).astype(F32), new], axis=1)
    return jnp.pad(new, ((0, 0), (0, l_pad - new.shape[1]), (0, 0)))


def _aug_queries(q, rows):
    qt = jnp.pad(jnp.transpose(q, (0, 2, 3, 1)), ((0, 0), (0, 0), (0, LANES - HEAD_DIM), (0, 0)))
    row = jnp.arange(LANES)[None, None, :, None]
    for e in range(N_EXTRA):
        qt = jnp.where(row == HEAD_DIM + e, rows[None, :, e, None, None], qt)
    return jnp.where(row == COL_FLAG, 1.0, qt).astype(BF16)


def _aug_keys(k, cols, lp):
    l = k.shape[1]
    lane = jnp.arange(LANES)
    body = jnp.pad(k, ((0, 0), (0, 0), (0, 0), (0, LANES - HEAD_DIM)))
    for e in range(N_EXTRA):
        body = jnp.where(lane == HEAD_DIM + e, cols[..., e:e + 1], body)
    body = jnp.pad(body, ((0, 0), (FRONT, lp - FRONT - l), (0, 0), (0, 0)))
    row = jnp.arange(lp)[None, :, None, None]
    is_pad = (row < FRONT) | (row >= FRONT + l)
    full = jnp.where(is_pad & (lane == COL_FLAG), NEG, body)
    return jnp.transpose(full, (0, 2, 1, 3)).astype(BF16)


def _aug_values(v, lp):
    l = v.shape[1]
    vt = jnp.pad(jnp.transpose(v, (0, 2, 3, 1)),
                 ((0, 0), (0, 0), (0, V_ROWS - HEAD_DIM), (FRONT, lp - FRONT - l)))
    row = jnp.arange(V_ROWS)[None, None, :, None]
    return jnp.where(row == HEAD_DIM, 1.0, vt).astype(BF16)


def _layer(x, past, lw, lam_init, t5_table, tm_proj, tm_post):
    (g_mix, w_in_p, b_forget, w_out, lam_q1, lam_k1, lam_q2, lam_k2, diff_subln,
     g_ffn, w_gate, w_up, w_down, conv_w, conv_b) = lw
    b, t, d = x.shape
    p_len = 0 if past is None else past[0].shape[1]
    l_keys = p_len + t
    l_pad = -(-l_keys // LANES) * LANES
    topk = min(TOPK_MAX, l_keys // 4)
    tq = min(t, TQ_MAX)
    assert p_len % FRONT == 0 and (t % TQ_MAX == 0 or t <= LANES) and FRONT == SB_KB == TQ_MAX
    lp = -(-max(FRONT + l_keys, p_len + t - tq + KB) // LANES) * LANES
    l_sb = p_len + t - tq + SB_KB
    (pk_sb, pv_sb, pk_df, pv_df, pk_ds, pv_ds, pk_ix, pk_fx, pv_fx, plogf, pconv) = (
        past if past is not None else (None,) * 11)

    bf = jnp.zeros((1, LANES), F32).at[0, SLAB_GATE:SLAB_GATE + N_HEADS].set(b_forget)
    y, lf = _proj(x.reshape(b * t, d), g_mix.reshape(1, d), w_in_p, bf, tm_proj)
    y = y.reshape(b, t, PROJ_PAD)
    grp = lambda i: y[:, :, i * GROUP_WIDTH:(i + 1) * GROUP_WIDTH]
    q_sb, k_sb, v_sb, q_df, k_df, v_df, q_ds, k_ds, v_ds = [grp(i) for i in range(9)]
    q_ix = y[:, :, COL_QIX:COL_SLAB]
    slab = y[:, :, COL_SLAB:COL_FX]
    k_ix = slab[:, :, :IDX_DIM]
    w_ix = slab[:, :, SLAB_WIX:SLAB_WIX + N_IDX_HEADS]
    q_fx, k_fx, v_fx = [y[:, :, COL_FX + i * GROUP_WIDTH:COL_FX + (i + 1) * GROUP_WIDTH] for i in range(3)]
    logf = lf.reshape(b, t, LANES)[:, :, SLAB_GATE:SLAB_GATE + N_HEADS]

    scale = HEAD_DIM ** -0.5
    hd = lambda a: a.reshape(b, -1, N_HEADS, HEAD_DIM)
    all_keys = lambda pst, new: hd(_with_past(pst, new, l_keys))
    near, far, causal = _near_tables(t5_table, tq)
    ones_cols = jnp.ones((b, l_keys, N_HEADS, N_EXTRA), F32)

    k_sb_all = jnp.transpose(hd(_with_past(pk_sb, k_sb, l_sb)), (0, 2, 1, 3)).astype(BF16)
    vt_sb_all = jnp.transpose(hd(_with_past(pv_sb, v_sb, l_sb)), (0, 2, 3, 1)).astype(BF16)
    later = (jnp.arange(SB_KB)[None, :] > jnp.arange(SB_KB)[:, None]).astype(BF16)
    o_sb = _sb_attn(jnp.transpose(hd(q_sb * scale), (0, 2, 3, 1)).astype(BF16), k_sb_all, vt_sb_all, later,
                    p_len, tq)

    half = jnp.arange(HEAD_DIM) < DIFF_DH
    q_df2 = jnp.where(jnp.stack([half, ~half])[None, None, None],
                      q_df.reshape(b, t, N_HEADS, 1, HEAD_DIM) * (DIFF_DH ** -0.5 * LOG2E), 0.0)
    lam_rows = jnp.zeros((SUBLANES, LANES), F32)
    for r, vec in enumerate((lam_q1, lam_k1, lam_q2, lam_k2)):
        lam_rows = lam_rows.at[r, :DIFF_DH].set(vec.astype(F32))
    lam_rows = lam_rows.at[4, :].set(lam_init)
    g_cols = jnp.broadcast_to(diff_subln.astype(F32)[:, None], (HEAD_DIM, tq))
    o_df = _df_attn(_aug_queries(q_df2.reshape(b, t, 2 * N_HEADS, HEAD_DIM),
                                 jnp.repeat(_split3(far[:N_HEADS]), 2, axis=0)),
                    _aug_keys(all_keys(pk_df, k_df), ones_cols, lp), _aug_values(all_keys(pv_df, v_df), lp),
                    near[:N_HEADS], lam_rows, g_cols, p_len, tq)

    kix_all = jnp.pad(_with_past(pk_ix, k_ix, l_keys), ((0, 0), (FRONT, lp - FRONT - l_keys), (0, 0))).astype(BF16)
    qix_t = jnp.transpose(q_ix.reshape(b, t, N_IDX_HEADS, IDX_DIM) * (IDX_DIM ** -0.5), (0, 2, 3, 1)).astype(BF16)
    upto = (jnp.arange(KB)[None, :] <= jnp.arange(KB)[:, None]).astype(BF16)
    o_ds = _ds_attn(qix_t, jnp.transpose(w_ix, (0, 2, 1)), kix_all,
                    _aug_queries(hd(q_ds * (scale * LOG2E)), _split3(far[N_HEADS:])),
                    _aug_keys(all_keys(pk_ds, k_ds), ones_cols, lp), _aug_values(all_keys(pv_ds, v_ds), lp),
                    near[N_HEADS:], upto, p_len, tq, l_keys, topk)

    c_all = _cumsum_seq(_with_past(plogf, logf, l_pad))
    c_keys = jnp.transpose(c_all[:, :, :l_keys], (0, 2, 1))
    o_fx = _fx_attn(_aug_queries(hd(q_fx * (scale * LOG2E)), jnp.ones((N_HEADS, N_EXTRA), F32)),
                    _aug_keys(all_keys(pk_fx, k_fx), _split3(-LOG2E * c_keys), lp),
                    _aug_values(all_keys(pv_fx, v_fx), lp), causal, p_len, tq)

    f = w_gate.shape[1]
    if pconv is None:
        pc = jnp.zeros((b, SUBLANES, f), F32)
    else:
        pc = jnp.pad(pconv.astype(F32), ((0, 0), (SUBLANES - (CONV_W - 1), 0), (0, 0)))
    cw = jnp.pad(conv_w.astype(F32), ((0, SUBLANES - CONV_W), (0, 0)))
    x_new, st = _post(x, (o_sb, o_df, o_ds, o_fx), w_out, g_ffn.reshape(1, d), w_gate, w_up,
                      w_down, cw, conv_b.reshape(1, f).astype(F32), pc, tm_post)

    rows = (hd(k_sb), hd(v_sb), k_df.reshape(b, t, N_HEADS, 2, DIFF_DH), hd(v_df), hd(k_ds), hd(v_ds),
            k_ix, hd(k_fx), hd(v_fx), logf, st[:, SUBLANES - (CONV_W - 1):])
    return x_new, rows


def _row_tile(n, cap):
    tm = min(n, cap)
    while n % tm:
        tm -= SUBLANES
    assert tm > 0
    return tm


def kernel(x_prompt, x_sample, cache_sb_k, cache_sb_v, cache_diff_k, cache_diff_v, cache_dsa_k, cache_dsa_v, cache_dsa_kidx, cache_fox_k, cache_fox_v, cache_fox_logf, state_ffn_conv, g_mix, w_in, b_forget, w_out, lam_q1, lam_k1, lam_q2, lam_k2, diff_subln, g_ffn, w_gate, w_up, w_down, conv_w, conv_b, t5_table, g_final):
    depth = w_in.shape[0]
    bp, tp, _ = x_prompt.shape
    bs, ts, _ = x_sample.shape
    xp, xs = x_prompt, x_sample
    p_rows, s_rows = [], []
    for l in range(depth):
        lw = (g_mix[l], _pack_w_in(w_in[l]), b_forget[l], w_out[l].astype(BF16), lam_q1[l], lam_k1[l],
              lam_q2[l], lam_k2[l], diff_subln[l], g_ffn[l], w_gate[l].astype(BF16), w_up[l].astype(BF16),
              w_down[l].astype(BF16), conv_w[l], conv_b[l])
        lam_init = 0.8 - 0.6 * math.exp(-0.3 * l)
        xp, rp = _layer(xp, None, lw, lam_init, t5_table, _row_tile(bp * tp, 1024), _row_tile(tp, 512))
        s_past = (cache_sb_k[l], cache_sb_v[l], cache_diff_k[l], cache_diff_v[l], cache_dsa_k[l],
                  cache_dsa_v[l], cache_dsa_kidx[l], cache_fox_k[l], cache_fox_v[l], cache_fox_logf[l],
                  state_ffn_conv[l])
        xs, rs = _layer(xs, s_past, lw, lam_init, t5_table, _row_tile(bs * ts, 512), _row_tile(ts, 256))
        p_rows.append(rp)
        s_rows.append(rs)
    y_prompt = _final_norm(xp, g_final, _row_tile(bp * tp, 512))
    y_sample = _final_norm(xs, g_final, _row_tile(bs * ts, 512))
    stack = lambda rows, i: jnp.stack([r[i] for r in rows], axis=0)
    return ((y_prompt, y_sample) + tuple(stack(p_rows, i) for i in range(11))
            + tuple(stack(s_rows, i) for i in range(11)))
```
